```python
import math, functools
import jax, jax.numpy as jnp
from jax import lax
import numpy as np

D_MODEL = 4096
BATCH = 8
SEQ = 2048
DEPTH = 2

HEAD_DIM = 128
A_W = D_MODEL // 4
A_HEADS = A_W // HEAD_DIM
HG_CHUNK = 64
B_W = D_MODEL // 4
B_BLOCKS = B_W // HEAD_DIM
B_BD = B_W // B_BLOCKS
RG_CONV = 4
RG_C = 8.0
C_HEADS = D_MODEL // 2 // HEAD_DIM
C_KV = C_HEADS // 4
C_GROUP = C_HEADS // C_KV
C_W = C_HEADS * HEAD_DIM
Q_BLOCK = 128
ROPE_THETA = 10000.0
ROPE_HALF = HEAD_DIM // 2
GRID_W = 64
D_MIX = A_W + B_W + C_W
A_COLS = 5 * A_W
B_COLS = 2 * B_W
C_COLS = C_W + 2 * C_KV * HEAD_DIM
IN_COLS = A_COLS + B_COLS + C_COLS
D_FF = ((8 * D_MODEL // 3 + 255) // 256) * 256
FFN_CONV = 3
DEEPNORM_ALPHA = (2.0 * DEPTH) ** 0.25
DEEPNORM_BETA = (8.0 * DEPTH) ** -0.25
LN_EPS = 1e-5
RMS_EPS = 1e-6

kernel_name = "hybrid_hgrn2_rglru_axialgqa_encoder"


def layer_norm(x, w, b):
    xf = x.astype(jnp.float32)
    mu = jnp.mean(xf, axis=-1, keepdims=True)
    var = jnp.mean(jnp.square(xf - mu), axis=-1, keepdims=True)
    y = (xf - mu) * lax.rsqrt(var + LN_EPS)
    return (y * w.astype(jnp.float32) + b.astype(jnp.float32)).astype(x.dtype)


def rms_norm(x, w):
    xf = x.astype(jnp.float32)
    y = xf * lax.rsqrt(jnp.mean(jnp.square(xf), axis=-1, keepdims=True) + RMS_EPS)
    return y * w.astype(jnp.float32)


def dwconv(x, w, b, left):
    k_w = w.shape[0]
    s = x.shape[1]
    xp = jnp.pad(x, ((0, 0), (left, k_w - 1 - left), (0, 0)))
    y = b
    for k in range(k_w):
        y = y + xp[:, k:k + s, :] * w[k]
    return y


def hgrn2_scan(q, logf, k, v):
    bsz, s, h, dk = q.shape
    dv = v.shape[-1]
    n_chunks = s // HG_CHUNK

    def to_chunks(t):
        return t.reshape(bsz, n_chunks, HG_CHUNK, h, t.shape[-1]).transpose(1, 0, 3, 2, 4)

    tri = jnp.tril(jnp.ones((HG_CHUNK, HG_CHUNK), dtype=bool))

    def step(state, xs):
        qc, lfc, kc, vc = xs
        b = jnp.cumsum(lfc, axis=2)
        b_last = b[:, :, -1:, :]
        o_inter = jnp.einsum('bhcd,bhde->bhce', qc * jnp.exp(b), state)
        diff = b[:, :, :, None, :] - b[:, :, None, :, :]
        decay = jnp.exp(jnp.where(tri[:, :, None], diff, -jnp.inf))
        scores = jnp.einsum('bhid,bhijd,bhjd->bhij', qc, decay, kc)
        o_intra = jnp.einsum('bhij,bhje->bhie', scores, vc)
        new_state = (jnp.exp(b_last)[:, :, 0, :, None] * state
                     + jnp.einsum('bhjd,bhje->bhde', kc * jnp.exp(b_last - b), vc))
        return new_state, o_inter + o_intra

    state0 = jnp.zeros((bsz, h, dk, dv), jnp.float32)
    _, out = lax.scan(step, state0, (to_chunks(q), to_chunks(logf), to_chunks(k), to_chunks(v)))
    return out.transpose(1, 0, 3, 2, 4).reshape(bsz, s, h, dv)


def hgrn2_mixer(u, lb, norm_w):
    bsz, s, _ = u.shape
    heads = lambda t: t.reshape(bsz, s, A_HEADS, HEAD_DIM).astype(jnp.float32)
    q = heads(u[..., 0 * A_W:1 * A_W])
    zf_fwd = heads(u[..., 1 * A_W:2 * A_W])
    zf_bwd = heads(u[..., 2 * A_W:3 * A_W])
    iv = heads(u[..., 3 * A_W:4 * A_W])
    g = u[..., 4 * A_W:5 * A_W]

    def gates(z, lb_dir):
        lbd = lb_dir.reshape(A_HEADS, HEAD_DIM)
        logf = jnp.logaddexp(jnp.log(lbd), jnp.log1p(-lbd) + jax.nn.log_sigmoid(z))
        kk = (1.0 - lbd) * jax.nn.sigmoid(-z)
        return logf, kk

    lf_f, k_f = gates(zf_fwd, lb[0])
    lf_b, k_b = gates(zf_bwd, lb[1])
    flip = lambda t: jnp.flip(t, axis=1)
    o_f = hgrn2_scan(q, lf_f, k_f, iv)
    o_b = flip(hgrn2_scan(flip(q), flip(lf_b), flip(k_b), flip(iv)))
    o = rms_norm(o_f + o_b, norm_w.reshape(A_HEADS, HEAD_DIM)).reshape(bsz, s, A_W)
    return (o * jax.nn.silu(g.astype(jnp.float32))).astype(u.dtype)


def _linear_rec_combine(left, right):
    a1, b1 = left
    a2, b2 = right
    return a1 * a2, a2 * b1 + b2


def rglru_mixer(u, conv_w, conv_b, wa, ba, wx, bx, lam):
    bsz, s, _ = u.shape
    xb = u[..., :B_W]
    gate = u[..., B_W:]
    xc = dwconv(xb, conv_w, conv_b, left=RG_CONV // 2).astype(jnp.float32)
    xh = xc.reshape(bsz, s, B_BLOCKS, B_BD)

    def direction(d, reverse):
        r = jax.nn.sigmoid(jnp.einsum('bsnd,nde->bsne', xh, wa[d].astype(jnp.float32)).reshape(bsz, s, B_W)
                           + ba[d].astype(jnp.float32))
        ig = jax.nn.sigmoid(jnp.einsum('bsnd,nde->bsne', xh, wx[d].astype(jnp.float32)).reshape(bsz, s, B_W)
                            + bx[d].astype(jnp.float32))
        log_a = -RG_C * r * jax.nn.softplus(-lam[d].astype(jnp.float32))
        a = jnp.exp(log_a)
        bterm = jnp.sqrt(-jnp.expm1(2.0 * log_a)) * (ig * xc)
        _, hs = lax.associative_scan(_linear_rec_combine, (a, bterm), axis=1, reverse=reverse)
        return hs

    hsum = direction(0, False) + direction(1, True)
    return (jax.nn.gelu(gate.astype(jnp.float32)) * hsum).astype(u.dtype)


def axial_rope(t, cos, sin):
    q4 = ROPE_HALF // 2
    tr = t[..., :ROPE_HALF]
    tc = t[..., ROPE_HALF:]
    rot = jnp.concatenate([-tr[..., q4:], tr[..., :q4], -tc[..., q4:], tc[..., :q4]], axis=-1)
    return t * cos[:, None, :] + rot * sin[:, None, :]


def gqa_mixer(u, qn_w, kn_w, cos, sin):
    bsz, s, _ = u.shape
    kv_w = C_KV * HEAD_DIM
    q = u[..., :C_W].reshape(bsz, s, C_HEADS, HEAD_DIM)
    k = u[..., C_W:C_W + kv_w].reshape(bsz, s, C_KV, HEAD_DIM)
    v = u[..., C_W + kv_w:].reshape(bsz, s, C_KV, HEAD_DIM).astype(jnp.float32)
    q = axial_rope(rms_norm(q, qn_w), cos, sin)
    k = axial_rope(rms_norm(k, kn_w), cos, sin)
    scale = HEAD_DIM ** -0.5
    n_blk = s // Q_BLOCK
    qb = q.reshape(bsz, n_blk, Q_BLOCK, C_KV, C_GROUP, HEAD_DIM).transpose(1, 0, 2, 3, 4, 5)

    def attend(q_blk):
        sc = jnp.einsum('bqkgd,bskd->bkgqs', q_blk, k) * scale
        p = jax.nn.softmax(sc, axis=-1)
        return jnp.einsum('bkgqs,bskd->bqkgd', p, v)

    out = lax.map(attend, qb)
    return out.transpose(1, 0, 2, 3, 4, 5).reshape(bsz, s, C_W).astype(u.dtype)


def setup_inputs(seed: int = 0) -> dict:
    key = jax.random.key(seed)
    ks = jax.random.split(key, 28)
    f32 = jnp.float32

    def nrm(k, shape, scale):
        return jax.random.normal(k, shape, f32) * scale

    a8 = jax.random.uniform(ks[12], (DEPTH, 2, B_W), f32, 0.9, 0.999)
    a_base = a8 ** (1.0 / RG_C)
    rglru_lambda = jnp.log(a_base) - jnp.log1p(-a_base)
    return {
        "x": nrm(ks[0], (BATCH, SEQ, D_MODEL), 1.0),
        "emb_ln_w": 1.0 + nrm(ks[1], (D_MODEL,), 0.02),
        "emb_ln_b": nrm(ks[2], (D_MODEL,), 0.02),
        "w_in": nrm(ks[3], (DEPTH, D_MODEL, IN_COLS), D_MODEL ** -0.5),
        "hgrn_lb_logits": nrm(ks[4], (DEPTH, 2, A_W), 0.5),
        "hgrn_norm_w": 1.0 + nrm(ks[5], (DEPTH, A_W), 0.02),
        "rglru_conv_w": nrm(ks[6], (DEPTH, RG_CONV, B_W), RG_CONV ** -0.5),
        "rglru_conv_b": nrm(ks[7], (DEPTH, B_W), 0.02),
        "rglru_wa": nrm(ks[8], (DEPTH, 2, B_BLOCKS, B_BD, B_BD), B_BD ** -0.5),
        "rglru_ba": nrm(ks[9], (DEPTH, 2, B_W), 0.02),
        "rglru_wx": nrm(ks[10], (DEPTH, 2, B_BLOCKS, B_BD, B_BD), B_BD ** -0.5),
        "rglru_bx": nrm(ks[11], (DEPTH, 2, B_W), 0.02),
        "rglru_lambda": rglru_lambda,
        "attn_q_norm_w": 1.0 + nrm(ks[13], (DEPTH, HEAD_DIM), 0.02),
        "attn_k_norm_w": 1.0 + nrm(ks[14], (DEPTH, HEAD_DIM), 0.02),
        "w_out": nrm(ks[15], (DEPTH, D_MIX, D_MODEL), D_MIX ** -0.5 * DEEPNORM_BETA),
        "ln1_w": 1.0 + nrm(ks[16], (DEPTH, D_MODEL), 0.02),
        "ln1_b": nrm(ks[17], (DEPTH, D_MODEL), 0.02),
        "ffn_w_up": nrm(ks[18], (DEPTH, D_MODEL, 2 * D_FF), D_MODEL ** -0.5),
        "ffn_conv_w": nrm(ks[19], (DEPTH, FFN_CONV, D_FF), FFN_CONV ** -0.5),
        "ffn_conv_b": nrm(ks[20], (DEPTH, D_FF), 0.02),
        "ffn_w_down": nrm(ks[21], (DEPTH, D_FF, D_MODEL), D_FF ** -0.5 * DEEPNORM_BETA),
        "ln2_w": 1.0 + nrm(ks[22], (DEPTH, D_MODEL), 0.02),
        "ln2_b": nrm(ks[23], (DEPTH, D_MODEL), 0.02),
    }


def reference(x, emb_ln_w, emb_ln_b, w_in, hgrn_lb_logits, hgrn_norm_w, rglru_conv_w, rglru_conv_b,
              rglru_wa, rglru_ba, rglru_wx, rglru_bx, rglru_lambda, attn_q_norm_w, attn_k_norm_w,
              w_out, ln1_w, ln1_b, ffn_w_up, ffn_conv_w, ffn_conv_b, ffn_w_down, ln2_w, ln2_b):
    s = x.shape[1]
    rows = s // GRID_W
    g_r, g_c = jnp.meshgrid(jnp.arange(rows), jnp.arange(GRID_W), indexing='ij')
    row = g_r.reshape(s).astype(jnp.float32)
    col = g_c.reshape(s).astype(jnp.float32)
    inv_freq = ROPE_THETA ** (-jnp.arange(0, ROPE_HALF, 2, dtype=jnp.float32) / ROPE_HALF)
    ang_r = row[:, None] * inv_freq[None, :]
    ang_c = col[:, None] * inv_freq[None, :]
    ang = jnp.concatenate([ang_r, ang_r, ang_c, ang_c], axis=-1)
    cos, sin = jnp.cos(ang), jnp.sin(ang)

    lb_cs = jnp.cumsum(jax.nn.softmax(hgrn_lb_logits.astype(jnp.float32), axis=0), axis=0)
    lower_bounds = lb_cs - lb_cs[0:1]

    h = layer_norm(x, emb_ln_w, emb_ln_b)
    for l in range(DEPTH):
        u = h @ w_in[l]
        u_a = u[..., :A_COLS]
        u_b = u[..., A_COLS:A_COLS + B_COLS]
        u_c = u[..., A_COLS + B_COLS:]
        y_a = hgrn2_mixer(u_a, lower_bounds[l], hgrn_norm_w[l])
        y_b = rglru_mixer(u_b, rglru_conv_w[l], rglru_conv_b[l], rglru_wa[l], rglru_ba[l],
                          rglru_wx[l], rglru_bx[l], rglru_lambda[l])
        y_c = gqa_mixer(u_c, attn_q_norm_w[l], attn_k_norm_w[l], cos, sin)
        mix = jnp.concatenate([y_a, y_b, y_c], axis=-1) @ w_out[l]
        h = layer_norm(DEEPNORM_ALPHA * h + mix, ln1_w[l], ln1_b[l])
        gu = h @ ffn_w_up[l]
        gate = jax.nn.silu(dwconv(gu[..., :D_FF], ffn_conv_w[l], ffn_conv_b[l], left=FFN_CONV // 2))
        ffn = (gate * gu[..., D_FF:]) @ ffn_w_down[l]
        h = layer_norm(DEEPNORM_ALPHA * h + ffn, ln2_w[l], ln2_b[l])
    return h
```

```python
import functools
import math

import jax
import jax.numpy as jnp
from jax import lax
from jax.experimental import pallas as pl
from jax.experimental.pallas import tpu as pltpu

F32 = jnp.float32
BF16 = jnp.bfloat16

HEAD_DIM = 128
RG_C = 8.0
C_GROUP = 4
ROPE_THETA = 10000.0
ROPE_HALF = HEAD_DIM // 2
GRID_W = 64
LN_EPS = 1e-5
RMS_EPS = 1e-6
ATTN_SCALE = HEAD_DIM ** -0.5

V7X_VMEM_BYTES = 64 * 1024 * 1024
LANES = 128
SUBLANES = 8

HG_CHUNK = 128
LN_ROWS = 256
MM_TM = 1024
MM_TN = 1024
OUT_TM = 512
FFN_TM = 1024
FFN_TN = 256
DOWN_TM = 512
DOWN_TN = 512
ATTN_TQ = 256
PREP_ROWS = 512
RG_ROWS = 512
SCAN_ROWS = 128
SCAN_LANES = 512


def _vmem_limit(nbytes):
    return int(min(V7X_VMEM_BYTES - 4 * 1024 * 1024, max(nbytes, 16 * 1024 * 1024)))


def _params(sem, vmem_bytes):
    return pltpu.CompilerParams(dimension_semantics=sem, vmem_limit_bytes=_vmem_limit(vmem_bytes))


def _dot(a, b):
    return jnp.dot(a, b, preferred_element_type=F32)


def _dot_nt(a, b):
    return lax.dot_general(a, b, (((1,), (1,)), ((), ())), preferred_element_type=F32)


def _dot_tn(a, b):
    return lax.dot_general(a, b, (((0,), (0,)), ((), ())), preferred_element_type=F32)


def _sigmoid(x):
    return 1.0 / (1.0 + jnp.exp(-x))


def _ln_kernel(x_ref, w_ref, b_ref, o_ref, obf_ref):
    x = x_ref[...]
    mu = jnp.mean(x, axis=-1, keepdims=True)
    xc = x - mu
    var = jnp.mean(xc * xc, axis=-1, keepdims=True)
    y = xc * lax.rsqrt(var + LN_EPS) * w_ref[...] + b_ref[...]
    o_ref[...] = y
    obf_ref[...] = y.astype(BF16)


def _layernorm(x, w, b):
    t, d = x.shape
    tr = min(LN_ROWS, t)
    return pl.pallas_call(
        _ln_kernel,
        grid=(t // tr,),
        in_specs=[pl.BlockSpec((tr, d), lambda i: (i, 0)),
                  pl.BlockSpec((1, d), lambda i: (0, 0)),
                  pl.BlockSpec((1, d), lambda i: (0, 0))],
        out_specs=[pl.BlockSpec((tr, d), lambda i: (i, 0)),
                   pl.BlockSpec((tr, d), lambda i: (i, 0))],
        out_shape=[jax.ShapeDtypeStruct((t, d), F32), jax.ShapeDtypeStruct((t, d), BF16)],
        compiler_params=_params(("parallel",), 2 * tr * d * 10 + (8 << 20)),
        name="layernorm",
    )(x, w.reshape(1, d), b.reshape(1, d))


def _mm_kernel(x_ref, w_ref, o_ref):
    o_ref[...] = _dot(x_ref[...], w_ref[...]).astype(o_ref.dtype)


def _matmul(x, w, tm, tn, out_dtype, name, n_cols=None):
    m, k = x.shape
    n = w.shape[1] if n_cols is None else n_cols
    tm, tn = min(tm, m), min(tn, n)
    vm = 2 * (tm * k * 2 + k * tn * 2 + tm * tn * 4) + tm * tn * 4 + (4 << 20)
    return pl.pallas_call(
        _mm_kernel,
        grid=(m // tm, n // tn),
        in_specs=[pl.BlockSpec((tm, k), lambda i, j: (i, 0)),
                  pl.BlockSpec((k, tn), lambda i, j: (0, j))],
        out_specs=pl.BlockSpec((tm, tn), lambda i, j: (i, j)),
        out_shape=jax.ShapeDtypeStruct((m, n), out_dtype),
        compiler_params=_params(("parallel", "arbitrary"), vm),
        name=name,
    )(x, w)


def _mm_res_kernel(x_ref, w_ref, r_ref, o_ref, *, alpha):
    o_ref[...] = alpha * r_ref[...] + _dot(x_ref[...], w_ref[...])


def _matmul_residual(x, w, res, alpha, tm, tn, name):
    m, k = x.shape
    n = w.shape[1]
    tm, tn = min(tm, m), min(tn, n)
    vm = 2 * (tm * k * 2 + k * tn * 2 + 2 * tm * tn * 4) + tm * tn * 4 + (4 << 20)
    return pl.pallas_call(
        functools.partial(_mm_res_kernel, alpha=alpha),
        grid=(m // tm, n // tn),
        in_specs=[pl.BlockSpec((tm, k), lambda i, j: (i, 0)),
                  pl.BlockSpec((k, tn), lambda i, j: (0, j)),
                  pl.BlockSpec((tm, tn), lambda i, j: (i, j))],
        out_specs=pl.BlockSpec((tm, tn), lambda i, j: (i, j)),
        out_shape=jax.ShapeDtypeStruct((m, n), F32),
        compiler_params=_params(("parallel", "arbitrary"), vm),
        name=name,
    )(x, w, res)


def _hgrn_chunk(q, z, v, lb, st_ref, rev, row, pair_xor, tri):
    c_len = q.shape[0]
    n_lvl = c_len.bit_length() - 1
    log_lb = jnp.log(lb)
    ls = jnp.minimum(z, 0.0) - jnp.log1p(jnp.exp(-jnp.abs(z)))
    bb = jnp.log1p(-lb) + ls
    mx = jnp.maximum(log_lb, bb)
    logf = mx + jnp.log(jnp.exp(log_lb - mx) + jnp.exp(bb - mx))
    kk = (1.0 - lb) * jnp.exp(ls - z)

    c = jnp.dot(tri, logf, preferred_element_type=F32, precision=lax.Precision.HIGHEST)
    c_tot = c[0:1, :] if rev else c[c_len - 1:c_len, :]

    qb = q.astype(BF16)
    vb = v.astype(BF16)
    scores = jnp.where(pair_xor == 0, _dot_nt(qb, kk.astype(BF16)), 0.0)
    e = c
    s = c - logf
    for lvl in range(n_lvl):
        m = 1 << lvl
        bit = (row >> lvl) & 1
        later = (bit == 0) if rev else (bit == 1)
        qm = jnp.where(later, q * jnp.exp(c - s), 0.0).astype(BF16)
        km = jnp.where(later, 0.0, kk * jnp.exp(e - c)).astype(BF16)
        scores = scores + jnp.where(pair_xor < 2 * m, _dot_nt(qm, km), 0.0)
        if lvl + 1 < n_lvl:
            from_earlier, from_later = (c_len - m, m) if rev else (m, c_len - m)
            s = jnp.where(later, pltpu.roll(s, from_earlier, 0), s)
            e = jnp.where(later, e, pltpu.roll(e, from_later, 0))

    st = st_ref[...]
    o = _dot_nt((q * jnp.exp(c)).astype(BF16), st.astype(BF16)) + _dot(scores.astype(BF16), vb)
    st_ref[...] = jnp.exp(c_tot) * st + _dot_tn(vb, (kk * jnp.exp(c_tot - c)).astype(BF16))
    return o


def _hgrn_kernel(q_ref, zf_ref, zb_ref, v_ref, g_ref, lb_ref, nw_ref, o_ref,
                 of_scr, ob_scr, stf_scr, stb_scr, *, chunk):
    s_len = q_ref.shape[0]
    n_chunks = s_len // chunk
    row = lax.broadcasted_iota(jnp.int32, (chunk, HEAD_DIM), 0)
    ri = lax.broadcasted_iota(jnp.int32, (chunk, chunk), 0)
    ci = lax.broadcasted_iota(jnp.int32, (chunk, chunk), 1)
    pair_xor = ri ^ ci
    tri_f = (ri >= ci).astype(F32)
    tri_b = (ci >= ri).astype(F32)
    lb_f = lb_ref[0:1, :]
    lb_b = lb_ref[1:2, :]
    stf_scr[...] = jnp.zeros_like(stf_scr)
    stb_scr[...] = jnp.zeros_like(stb_scr)

    def body(i, carry):
        rf = pl.multiple_of(i * chunk, chunk)
        of_scr[pl.ds(rf, chunk), :] = _hgrn_chunk(
            q_ref[pl.ds(rf, chunk), :], zf_ref[pl.ds(rf, chunk), :], v_ref[pl.ds(rf, chunk), :],
            lb_f, stf_scr, False, row, pair_xor, tri_f)
        rb = pl.multiple_of((n_chunks - 1 - i) * chunk, chunk)
        ob_scr[pl.ds(rb, chunk), :] = _hgrn_chunk(
            q_ref[pl.ds(rb, chunk), :], zb_ref[pl.ds(rb, chunk), :], v_ref[pl.ds(rb, chunk), :],
            lb_b, stb_scr, True, row, pair_xor, tri_b)
        return carry

    lax.fori_loop(0, n_chunks, body, 0)

    nw = nw_ref[...]

    def epilogue(i, carry):
        r = pl.multiple_of(i * chunk, chunk)
        o = of_scr[pl.ds(r, chunk), :] + ob_scr[pl.ds(r, chunk), :]
        y = o * lax.rsqrt(jnp.mean(o * o, axis=-1, keepdims=True) + RMS_EPS) * nw
        g = g_ref[pl.ds(r, chunk), :]
        o_ref[pl.ds(r, chunk), :] = (y * (g * _sigmoid(g))).astype(o_ref.dtype)
        return carry

    lax.fori_loop(0, n_chunks, epilogue, 0)


def _hgrn2(u3, lb, norm_w, a_w):
    bsz, s_len, _ = u3.shape
    heads = a_w // HEAD_DIM
    chunk = min(HG_CHUNK, s_len)

    def col(k):
        return pl.BlockSpec((None, s_len, HEAD_DIM), lambda b, h: (b, 0, k * heads + h))

    out = pl.pallas_call(
        functools.partial(_hgrn_kernel, chunk=chunk),
        grid=(bsz, heads),
        in_specs=[col(0), col(1), col(2), col(3), col(4),
                  pl.BlockSpec((2, HEAD_DIM), lambda b, h: (0, h)),
                  pl.BlockSpec((1, HEAD_DIM), lambda b, h: (0, h))],
        out_specs=pl.BlockSpec((None, s_len, HEAD_DIM), lambda b, h: (b, 0, h)),
        out_shape=jax.ShapeDtypeStruct((bsz, s_len, a_w), BF16),
        scratch_shapes=[pltpu.VMEM((s_len, HEAD_DIM), F32), pltpu.VMEM((s_len, HEAD_DIM), F32),
                        pltpu.VMEM((HEAD_DIM, HEAD_DIM), F32), pltpu.VMEM((HEAD_DIM, HEAD_DIM), F32)],
        compiler_params=_params(("parallel", "arbitrary"), 14 * s_len * HEAD_DIM * 4 + (16 << 20)),
        name="hgrn2",
    )(u3, u3, u3, u3, u3, lb, norm_w.reshape(1, a_w))
    return out.reshape(bsz * s_len, a_w)


def _rglru_gate_kernel(x_ref, xp_ref, xn_ref, cw_ref, cb_ref, w_ref, bias_ref, lam_ref,
                       af_ref, bf_ref, ab_ref, bb_ref):
    ts, width = x_ref.shape
    i = pl.program_id(1)
    has_prev = (i > 0).astype(F32)
    has_next = (i < pl.num_programs(1) - 1).astype(F32)
    x = x_ref[...]
    row = lax.broadcasted_iota(jnp.int32, (ts, width), 0)
    p6 = xp_ref[SUBLANES - 2:SUBLANES - 1, :] * has_prev
    p7 = xp_ref[SUBLANES - 1:SUBLANES, :] * has_prev
    n0 = xn_ref[0:1, :] * has_next
    xm1 = jnp.where(row == 0, p7, pltpu.roll(x, 1, 0))
    xm2 = jnp.where(row == 0, p6, jnp.where(row == 1, p7, pltpu.roll(x, 2, 0)))
    xp1 = jnp.where(row == ts - 1, n0, pltpu.roll(x, ts - 1, 0))
    xc = (cb_ref[...] + cw_ref[0:1, :] * xm2 + cw_ref[1:2, :] * xm1
          + cw_ref[2:3, :] * x + cw_ref[3:4, :] * xp1)
    lam = lam_ref[...]
    sp = jnp.maximum(-lam, 0.0) + jnp.log1p(jnp.exp(-jnp.abs(lam)))
    outs = ((af_ref, bf_ref), (ab_ref, bb_ref))
    for n in range(width // HEAD_DIM):
        sl = slice(n * HEAD_DIM, (n + 1) * HEAD_DIM)
        xn_blk = xc[:, sl]
        zz = _dot(xn_blk.astype(BF16), w_ref[n])
        for d in range(2):
            r = _sigmoid(zz[:, (2 * d) * HEAD_DIM:(2 * d + 1) * HEAD_DIM] + bias_ref[2 * d:2 * d + 1, sl])
            ig = _sigmoid(zz[:, (2 * d + 1) * HEAD_DIM:(2 * d + 2) * HEAD_DIM]
                          + bias_ref[2 * d + 1:2 * d + 2, sl])
            log_a = (-RG_C) * r * sp[d:d + 1, sl]
            a = jnp.exp(log_a)
            mult = jnp.sqrt(-jnp.tanh(log_a) * (a * a + 1.0))
            outs[d][0][:, sl] = a
            outs[d][1][:, sl] = mult * (ig * xn_blk)


def _rglru_gates(u3, col0, conv_w, conv_b, w_packed, bias, lam, b_w):
    bsz, s_len, _ = u3.shape
    ts = min(RG_ROWS, s_len)
    nt = s_len // ts
    cb = col0 // b_w
    rows8 = ts // SUBLANES
    out_sd = jax.ShapeDtypeStruct((s_len, bsz * b_w), F32)
    out_spec = pl.BlockSpec((ts, b_w), lambda b, i: (i, b))
    return pl.pallas_call(
        _rglru_gate_kernel,
        grid=(bsz, nt),
        in_specs=[pl.BlockSpec((None, ts, b_w), lambda b, i: (b, i, cb)),
                  pl.BlockSpec((None, SUBLANES, b_w), lambda b, i: (b, jnp.maximum(i * rows8 - 1, 0), cb)),
                  pl.BlockSpec((None, SUBLANES, b_w),
                               lambda b, i: (b, jnp.minimum((i + 1) * rows8, s_len // SUBLANES - 1), cb)),
                  pl.BlockSpec((4, b_w), lambda b, i: (0, 0)),
                  pl.BlockSpec((1, b_w), lambda b, i: (0, 0)),
                  pl.BlockSpec(w_packed.shape, lambda b, i: (0, 0, 0)),
                  pl.BlockSpec((4, b_w), lambda b, i: (0, 0)),
                  pl.BlockSpec((2, b_w), lambda b, i: (0, 0))],
        out_specs=[out_spec, out_spec, out_spec, out_spec],
        out_shape=[out_sd, out_sd, out_sd, out_sd],
        compiler_params=_params(("parallel", "arbitrary"), 2 * 5 * ts * b_w * 4 + (16 << 20)),
        name="rglru_gates",
    )(u3, u3, u3, conv_w, conv_b.reshape(1, b_w), w_packed, bias, lam)


def _rglru_scan_kernel(af_ref, bf_ref, ab_ref, bb_ref, hf_ref, hb_ref, cf_scr, cb_scr):
    ts = af_ref.shape[0]

    @pl.when(pl.program_id(1) == 0)
    def _():
        cf_scr[...] = jnp.zeros_like(cf_scr)
        cb_scr[...] = jnp.zeros_like(cb_scr)

    def body(t, carry):
        hf, hb = carry
        hf = af_ref[t] * hf + bf_ref[t]
        hf_ref[t] = hf
        tb = ts - 1 - t
        hb = ab_ref[tb] * hb + bb_ref[tb]
        hb_ref[tb] = hb
        return hf, hb

    hf, hb = lax.fori_loop(0, ts, body, (cf_scr[...], cb_scr[...]), unroll=8)
    cf_scr[...] = hf
    cb_scr[...] = hb


def _rglru_scan(af, bf, ab, bb, bsz, b_w):
    s_len = af.shape[0]
    ts = min(SCAN_ROWS, s_len)
    tl = min(SCAN_LANES, b_w)
    nt = s_len // ts
    shape3 = (s_len, bsz, b_w)
    fwd = pl.BlockSpec((ts, bsz, tl), lambda l, i: (i, 0, l))
    bwd = pl.BlockSpec((ts, bsz, tl), lambda l, i: (nt - 1 - i, 0, l))
    sd = jax.ShapeDtypeStruct(shape3, F32)
    hf, hb = pl.pallas_call(
        _rglru_scan_kernel,
        grid=(b_w // tl, nt),
        in_specs=[fwd, fwd, bwd, bwd],
        out_specs=[fwd, bwd],
        out_shape=[sd, sd],
        scratch_shapes=[pltpu.VMEM((bsz, tl), F32), pltpu.VMEM((bsz, tl), F32)],
        compiler_params=_params(("parallel", "arbitrary"), 2 * 6 * ts * bsz * tl * 4 + (8 << 20)),
        name="rglru_scan",
    )(af.reshape(shape3), bf.reshape(shape3), ab.reshape(shape3), bb.reshape(shape3))
    return hf.reshape(s_len, bsz * b_w), hb.reshape(s_len, bsz * b_w)


def _attn_prep_kernel(q0_ref, q1_ref, kv_ref, cos_ref, sa_ref, sb_ref, qn_ref, kn_ref,
                      qo_ref, ko_ref, vo_ref):
    cos = cos_ref[...]
    sin_a = sa_ref[...]
    sin_b = sb_ref[...]

    def norm_rope(t, w):
        y = t * lax.rsqrt(jnp.mean(t * t, axis=-1, keepdims=True) + RMS_EPS) * w
        return (y * cos + pltpu.roll(y, HEAD_DIM - ROPE_HALF // 2, 1) * sin_a
                + pltpu.roll(y, ROPE_HALF // 2, 1) * sin_b)

    half_heads = q0_ref.shape[1] // HEAD_DIM
    kv_heads = ko_ref.shape[1] // HEAD_DIM
    qn = qn_ref[...]
    kn = kn_ref[...]
    for hd in range(half_heads):
        sl = slice(hd * HEAD_DIM, (hd + 1) * HEAD_DIM)
        so = slice((half_heads + hd) * HEAD_DIM, (half_heads + hd + 1) * HEAD_DIM)
        qo_ref[:, sl] = norm_rope(q0_ref[:, sl], qn).astype(BF16)
        qo_ref[:, so] = norm_rope(q1_ref[:, sl], qn).astype(BF16)
    for hd in range(kv_heads):
        sl = slice(hd * HEAD_DIM, (hd + 1) * HEAD_DIM)
        ko_ref[:, sl] = norm_rope(kv_ref[:, sl], kn).astype(BF16)
    vo_ref[...] = kv_ref[:, kv_heads * HEAD_DIM:].astype(BF16)


def _attn_prep(u2, col0, c_w, kv_w, s_len, cos, sin_a, sin_b, qn_w, kn_w):
    t = u2.shape[0]
    tr = min(PREP_ROWS, s_len)
    half = c_w // 2
    assert col0 % half == 0 and 2 * kv_w == half
    cb = col0 // half
    npos = s_len // tr
    tab = pl.BlockSpec((tr, HEAD_DIM), lambda i: (i % npos, 0))
    vec = pl.BlockSpec((1, HEAD_DIM), lambda i: (0, 0))
    return pl.pallas_call(
        _attn_prep_kernel,
        grid=(t // tr,),
        in_specs=[pl.BlockSpec((tr, half), lambda i: (i, cb)),
                  pl.BlockSpec((tr, half), lambda i: (i, cb + 1)),
                  pl.BlockSpec((tr, half), lambda i: (i, cb + 2)),
                  tab, tab, tab, vec, vec],
        out_specs=[pl.BlockSpec((tr, c_w), lambda i: (i, 0)),
                   pl.BlockSpec((tr, kv_w), lambda i: (i, 0)),
                   pl.BlockSpec((tr, kv_w), lambda i: (i, 0))],
        out_shape=[jax.ShapeDtypeStruct((t, c_w), BF16),
                   jax.ShapeDtypeStruct((t, kv_w), BF16),
                   jax.ShapeDtypeStruct((t, kv_w), BF16)],
        compiler_params=_params(("parallel",), 2 * tr * (3 * half * 4 + (c_w + 2 * kv_w) * 2) + (16 << 20)),
        name="attn_prep",
    )(u2, u2, u2, cos, sin_a, sin_b, qn_w.reshape(1, HEAD_DIM), kn_w.reshape(1, HEAD_DIM))


def _attn_kernel(q_ref, k_ref, v_ref, o_ref):
    k = k_ref[...]
    v = v_ref[...]
    for g in range(C_GROUP):
        sl = slice(g * HEAD_DIM, (g + 1) * HEAD_DIM)
        sc = _dot_nt(q_ref[:, sl], k)
        mx = jnp.max(sc, axis=-1, keepdims=True)
        p = jnp.exp((sc - mx) * ATTN_SCALE)
        den = jnp.sum(p, axis=-1, keepdims=True)
        o_ref[:, sl] = (_dot(p.astype(BF16), v) / den).astype(o_ref.dtype)


def _attention(qr, kr, vr, bsz, s_len):
    t, c_w = qr.shape
    kv_heads = kr.shape[1] // HEAD_DIM
    gw = C_GROUP * HEAD_DIM
    tq = min(ATTN_TQ, s_len)
    nq = s_len // tq
    return pl.pallas_call(
        _attn_kernel,
        grid=(bsz, kv_heads, nq),
        in_specs=[pl.BlockSpec((tq, gw), lambda b, h, i: (b * nq + i, h)),
                  pl.BlockSpec((s_len, HEAD_DIM), lambda b, h, i: (b, h)),
                  pl.BlockSpec((s_len, HEAD_DIM), lambda b, h, i: (b, h))],
        out_specs=pl.BlockSpec((tq, gw), lambda b, h, i: (b * nq + i, h)),
        out_shape=jax.ShapeDtypeStruct((t, c_w), BF16),
        compiler_params=_params(("parallel", "parallel", "arbitrary"),
                                4 * tq * s_len * 4 + 4 * s_len * HEAD_DIM * 2 + (16 << 20)),
        name="attention",
    )(qr, kr, vr)


def _gelu_tanh(x):
    return 0.5 * x * (1.0 + jnp.tanh(math.sqrt(2.0 / math.pi) * (x + 0.044715 * (x * x * x))))


def _outproj_kernel(ya_ref, hf_ref, hb_ref, gate_ref, yc_ref, wa_ref, wb_ref, wc_ref, res_ref,
                    o_ref, yb_scr, *, alpha):
    @pl.when(pl.program_id(1) == 0)
    def _():
        yb_scr[...] = (_gelu_tanh(gate_ref[...]) * (hf_ref[...] + hb_ref[...])).astype(BF16)

    acc = _dot(ya_ref[...], wa_ref[...])
    acc = acc + _dot(yb_scr[...], wb_ref[...])
    acc = acc + _dot(yc_ref[...], wc_ref[...])
    o_ref[...] = alpha * res_ref[...] + acc


def _outproj(ya, hf, hb, u2, gate_col0, yc, w_out, res, alpha, s_len):
    t, a_w = ya.shape
    b_w = hf.shape[1] // (t // s_len)
    c_w = yc.shape[1]
    n = w_out.shape[1]
    assert a_w == b_w and c_w % a_w == 0 and gate_col0 % b_w == 0
    tm = min(OUT_TM, s_len)
    tn = min(MM_TN, n)
    ns = s_len // tm
    gcb = gate_col0 // b_w
    c_blk = (a_w + b_w) // c_w
    assert c_blk * c_w == a_w + b_w
    vm = 2 * (tm * a_w * 2 + 3 * tm * b_w * 4 + tm * c_w * 2 + (a_w + b_w + c_w) * tn * 2
              + 2 * tm * tn * 4) + tm * b_w * 2 + tm * tn * 4 + (4 << 20)
    return pl.pallas_call(
        functools.partial(_outproj_kernel, alpha=alpha),
        grid=(t // tm, n // tn),
        in_specs=[pl.BlockSpec((tm, a_w), lambda i, j: (i, 0)),
                  pl.BlockSpec((tm, b_w), lambda i, j: (i % ns, i // ns)),
                  pl.BlockSpec((tm, b_w), lambda i, j: (i % ns, i // ns)),
                  pl.BlockSpec((tm, b_w), lambda i, j: (i, gcb)),
                  pl.BlockSpec((tm, c_w), lambda i, j: (i, 0)),
                  pl.BlockSpec((a_w, tn), lambda i, j: (0, j)),
                  pl.BlockSpec((b_w, tn), lambda i, j: (1, j)),
                  pl.BlockSpec((c_w, tn), lambda i, j: (c_blk, j)),
                  pl.BlockSpec((tm, tn), lambda i, j: (i, j))],
        out_specs=pl.BlockSpec((tm, tn), lambda i, j: (i, j)),
        out_shape=jax.ShapeDtypeStruct((t, n), F32),
        scratch_shapes=[pltpu.VMEM((tm, b_w), BF16)],
        compiler_params=_params(("parallel", "arbitrary"), vm),
        name="outproj",
    )(ya, hf, hb, u2, yc, w_out, w_out, w_out, res)


def _ffn_up_kernel(x_ref, wg_ref, wu_ref, gp_ref, gn_ref, cw_ref, cb_ref, o_ref, *, tiles_per_seq):
    tm, tn = o_ref.shape
    x = x_ref[...]
    g = _dot(x, wg_ref[...])
    up = _dot(x, wu_ref[...])
    pos = pl.program_id(0) % tiles_per_seq
    has_prev = (pos > 0).astype(F32)
    has_next = (pos < tiles_per_seq - 1).astype(F32)
    g_prev = gp_ref[0, 1:2, :] * has_prev
    g_next = gn_ref[0, 0:1, :] * has_next
    row = lax.broadcasted_iota(jnp.int32, (tm, tn), 0)
    g_m1 = jnp.where(row == 0, g_prev, pltpu.roll(g, 1, 0))
    g_p1 = jnp.where(row == tm - 1, g_next, pltpu.roll(g, tm - 1, 0))
    y = cb_ref[...] + cw_ref[0:1, :] * g_m1 + cw_ref[1:2, :] * g + cw_ref[2:3, :] * g_p1
    o_ref[...] = ((y * _sigmoid(y)) * up).astype(o_ref.dtype)


def _ffn_up(hbf, w_up, conv_w, conv_b, s_len):
    t, d = hbf.shape
    d_ff = w_up.shape[1] // 2
    tm = min(FFN_TM, s_len)
    tn = FFN_TN
    assert d_ff % tn == 0
    nj = d_ff // tn
    n_tiles = t // tm
    tiles_per_seq = s_len // tm
    edge_x = hbf.reshape(n_tiles, tm, d)[:, (0, tm - 1), :].reshape(2 * n_tiles, d)
    g_edge = _matmul(edge_x, w_up, 2 * n_tiles, tn, F32, "ffn_edge", n_cols=d_ff).reshape(n_tiles, 2, d_ff)
    vm = 2 * (tm * d * 2 + 2 * d * tn * 2 + tm * tn * 2) + 8 * tm * tn * 4 + (4 << 20)
    return pl.pallas_call(
        functools.partial(_ffn_up_kernel, tiles_per_seq=tiles_per_seq),
        grid=(n_tiles, nj),
        in_specs=[pl.BlockSpec((tm, d), lambda i, j: (i, 0)),
                  pl.BlockSpec((d, tn), lambda i, j: (0, j)),
                  pl.BlockSpec((d, tn), lambda i, j: (0, nj + j)),
                  pl.BlockSpec((1, 2, tn), lambda i, j: (jnp.maximum(i - 1, 0), 0, j)),
                  pl.BlockSpec((1, 2, tn), lambda i, j: (jnp.minimum(i + 1, n_tiles - 1), 0, j)),
                  pl.BlockSpec((3, tn), lambda i, j: (0, j)),
                  pl.BlockSpec((1, tn), lambda i, j: (0, j))],
        out_specs=pl.BlockSpec((tm, tn), lambda i, j: (i, j)),
        out_shape=jax.ShapeDtypeStruct((t, d_ff), BF16),
        compiler_params=_params(("parallel", "arbitrary"), vm),
        name="ffn_up",
    )(hbf, w_up, w_up, g_edge, g_edge, conv_w, conv_b.reshape(1, d_ff))


def _rope_tables(s_len):
    rows = s_len // GRID_W
    g_r, g_c = jnp.meshgrid(jnp.arange(rows), jnp.arange(GRID_W), indexing='ij')
    row = g_r.reshape(s_len).astype(F32)
    colp = g_c.reshape(s_len).astype(F32)
    inv_freq = ROPE_THETA ** (-jnp.arange(0, ROPE_HALF, 2, dtype=F32) / ROPE_HALF)
    ang_r = row[:, None] * inv_freq[None, :]
    ang_c = colp[:, None] * inv_freq[None, :]
    ang = jnp.concatenate([ang_r, ang_r, ang_c, ang_c], axis=-1)
    cos, sin = jnp.cos(ang), jnp.sin(ang)
    first_quarter = (jnp.arange(HEAD_DIM) & (ROPE_HALF // 2)) == 0
    sin_a = jnp.where(first_quarter, -sin, 0.0)
    sin_b = jnp.where(first_quarter, 0.0, sin)
    return cos, sin_a, sin_b


def kernel(x, emb_ln_w, emb_ln_b, w_in, hgrn_lb_logits, hgrn_norm_w, rglru_conv_w, rglru_conv_b,
           rglru_wa, rglru_ba, rglru_wx, rglru_bx, rglru_lambda, attn_q_norm_w, attn_k_norm_w,
           w_out, ln1_w, ln1_b, ffn_w_up, ffn_conv_w, ffn_conv_b, ffn_w_down, ln2_w, ln2_b):
    bsz, s_len, d_model = x.shape
    depth = w_in.shape[0]
    t = bsz * s_len
    a_w = hgrn_norm_w.shape[1]
    b_w = rglru_conv_w.shape[2]
    in_cols = w_in.shape[2]
    a_cols = 5 * a_w
    b_cols = 2 * b_w
    c_w = w_out.shape[1] - a_w - b_w
    kv_w = (in_cols - a_cols - b_cols - c_w) // 2
    alpha = (2.0 * depth) ** 0.25

    cos, sin_a, sin_b = _rope_tables(s_len)
    lb_cs = jnp.cumsum(jax.nn.softmax(hgrn_lb_logits.astype(F32), axis=0), axis=0)
    lower_bounds = lb_cs - lb_cs[0:1]

    h, hbf = _layernorm(x.reshape(t, d_model), emb_ln_w, emb_ln_b)
    for l in range(depth):
        u2 = _matmul(hbf, w_in[l].astype(BF16), MM_TM, MM_TN, F32, "in_proj")
        u3 = u2.reshape(bsz, s_len, in_cols)

        ya = _hgrn2(u3, lower_bounds[l], hgrn_norm_w[l], a_w)

        nb = b_w // HEAD_DIM
        w_packed = jnp.concatenate([rglru_wa[l, 0], rglru_wx[l, 0], rglru_wa[l, 1], rglru_wx[l, 1]],
                                   axis=-1).astype(BF16)
        assert w_packed.shape == (nb, HEAD_DIM, 4 * HEAD_DIM)
        bias = jnp.stack([rglru_ba[l, 0], rglru_bx[l, 0], rglru_ba[l, 1], rglru_bx[l, 1]])
        af, bf, ab, bb = _rglru_gates(u3, a_cols, rglru_conv_w[l], rglru_conv_b[l], w_packed, bias,
                                      rglru_lambda[l], b_w)
        hf, hb = _rglru_scan(af, bf, ab, bb, bsz, b_w)

        qr, kr, vr = _attn_prep(u2, a_cols + b_cols, c_w, kv_w, s_len, cos, sin_a, sin_b,
                                attn_q_norm_w[l], attn_k_norm_w[l])
        yc = _attention(qr, kr, vr, bsz, s_len)

        pre = _outproj(ya, hf, hb, u2, a_cols + b_w, yc, w_out[l].astype(BF16), h, alpha, s_len)
        h, hbf = _layernorm(pre, ln1_w[l], ln1_b[l])

        act = _ffn_up(hbf, ffn_w_up[l].astype(BF16), ffn_conv_w[l], ffn_conv_b[l], s_len)
        pre = _matmul_residual(act, ffn_w_down[l].astype(BF16), h, alpha, DOWN_TM, DOWN_TN, "ffn_down")
        h, hbf = _layernorm(pre, ln2_w[l], ln2_b[l])
    return h.reshape(bsz, s_len, d_model)
```

```python
import functools
import math

import jax
import jax.numpy as jnp
from jax import lax
from jax.experimental import pallas as pl
from jax.experimental.pallas import tpu as pltpu

F32 = jnp.float32
BF16 = jnp.bfloat16

HEAD_DIM = 128
RG_C = 8.0
C_GROUP = 4
ROPE_THETA = 10000.0
ROPE_HALF = HEAD_DIM // 2
GRID_W = 64
LN_EPS = 1e-5
RMS_EPS = 1e-6
ATTN_SCALE = HEAD_DIM ** -0.5
LOG2_E = math.log2(math.e)

V7X_VMEM_BYTES = 64 * 1024 * 1024
LANES = 128
SUBLANES = 8
BF16_ROWS = 16

HG_CHUNK = 128
HG_HEADS_PER_STEP = 2
LN_ROWS = 256
MM_TM = 1024
MM_TN = 1024
OUT_TM = 512
FFN_TM = 1024
FFN_TN = 256
DOWN_TM = 512
DOWN_TN = 512
ATTN_TQ = 512
ATTN_KB = 256
PREP_ROWS = 512
RG_ROWS = 512
SCAN_ROWS = 128
SCAN_LANES = 512


def _vmem_limit(nbytes):
    return int(min(V7X_VMEM_BYTES - 4 * 1024 * 1024, max(nbytes, 16 * 1024 * 1024)))


def _params(sem, vmem_bytes):
    return pltpu.CompilerParams(dimension_semantics=sem, vmem_limit_bytes=_vmem_limit(vmem_bytes))


def _dot(a, b):
    return jnp.dot(a, b, preferred_element_type=F32)


def _dot_nt(a, b):
    return lax.dot_general(a, b, (((1,), (1,)), ((), ())), preferred_element_type=F32)


def _dot_tn(a, b):
    return lax.dot_general(a, b, (((0,), (0,)), ((), ())), preferred_element_type=F32)


def _sigmoid(x):
    return 1.0 / (1.0 + jnp.exp(-x))


def _ln_kernel(x_ref, w_ref, b_ref, o_ref, obf_ref):
    x = x_ref[...]
    mu = jnp.mean(x, axis=-1, keepdims=True)
    xc = x - mu
    var = jnp.mean(xc * xc, axis=-1, keepdims=True)
    y = xc * lax.rsqrt(var + LN_EPS) * w_ref[...] + b_ref[...]
    o_ref[...] = y
    obf_ref[...] = y.astype(BF16)


def _layernorm(x, w, b):
    t, d = x.shape
    tr = min(LN_ROWS, t)
    return pl.pallas_call(
        _ln_kernel,
        grid=(t // tr,),
        in_specs=[pl.BlockSpec((tr, d), lambda i: (i, 0)),
                  pl.BlockSpec((1, d), lambda i: (0, 0)),
                  pl.BlockSpec((1, d), lambda i: (0, 0))],
        out_specs=[pl.BlockSpec((tr, d), lambda i: (i, 0)),
                   pl.BlockSpec((tr, d), lambda i: (i, 0))],
        out_shape=[jax.ShapeDtypeStruct((t, d), F32), jax.ShapeDtypeStruct((t, d), BF16)],
        compiler_params=_params(("parallel",), 2 * tr * d * 10 + (8 << 20)),
        name="layernorm",
    )(x, w.reshape(1, d), b.reshape(1, d))


def _mm_kernel(x_ref, w_ref, o_ref):
    o_ref[...] = _dot(x_ref[...], w_ref[...]).astype(o_ref.dtype)


def _matmul(x, w, layer, tm, tn, out_dtype, name):
    m, k = x.shape
    n = w.shape[2]
    tm, tn = min(tm, m), min(tn, n)
    vm = 2 * (tm * k * 2 + k * tn * 2 + tm * tn * 4) + tm * tn * 4 + (4 << 20)
    return pl.pallas_call(
        _mm_kernel,
        grid=(m // tm, n // tn),
        in_specs=[pl.BlockSpec((tm, k), lambda i, j: (i, 0)),
                  pl.BlockSpec((None, k, tn), lambda i, j: (layer, 0, j))],
        out_specs=pl.BlockSpec((tm, tn), lambda i, j: (i, j)),
        out_shape=jax.ShapeDtypeStruct((m, n), out_dtype),
        compiler_params=_params(("parallel", "arbitrary"), vm),
        name=name,
    )(x, w)


def _mm_res_kernel(x_ref, w_ref, r_ref, o_ref, *, alpha):
    o_ref[...] = alpha * r_ref[...] + _dot(x_ref[...], w_ref[...])


def _matmul_residual(x, w, layer, res, alpha, tm, tn, name):
    m, k = x.shape
    n = w.shape[2]
    tm, tn = min(tm, m), min(tn, n)
    vm = 2 * (tm * k * 2 + k * tn * 2 + 2 * tm * tn * 4) + tm * tn * 4 + (4 << 20)
    return pl.pallas_call(
        functools.partial(_mm_res_kernel, alpha=alpha),
        grid=(m // tm, n // tn),
        in_specs=[pl.BlockSpec((tm, k), lambda i, j: (i, 0)),
                  pl.BlockSpec((None, k, tn), lambda i, j: (layer, 0, j)),
                  pl.BlockSpec((tm, tn), lambda i, j: (i, j))],
        out_specs=pl.BlockSpec((tm, tn), lambda i, j: (i, j)),
        out_shape=jax.ShapeDtypeStruct((m, n), F32),
        compiler_params=_params(("parallel", "arbitrary"), vm),
        name=name,
    )(x, w, res)


def _hgrn_chunk(q, z, v, lb, st_ref, rev, row, pair_code):
    c_len = q.shape[0]
    n_lvl = c_len.bit_length() - 1
    t = jnp.exp(-jnp.abs(z))
    r = 1.0 / (1.0 + t)
    tr = t * r
    pos = z >= 0.0
    f = lb + (1.0 - lb) * jnp.where(pos, r, tr)
    kk = (1.0 - lb) * jnp.where(pos, tr, r)

    vb = v.astype(BF16)
    scores = _dot_nt(q.astype(BF16), kk.astype(BF16))
    ep = f
    ex = jnp.ones_like(f)
    et = f
    for lvl in range(n_lvl):
        m = 1 << lvl
        later = ((row & m) == 0) if rev else ((row & m) != 0)
        x = jnp.where(later, q * ep, kk * ex).astype(BF16)
        scores = jnp.where(pair_code >= m, _dot_nt(x, x), scores)
        from_earlier, from_later = (c_len - m, m) if rev else (m, c_len - m)
        sib = jnp.where(later, pltpu.roll(et, from_earlier, 0), pltpu.roll(et, from_later, 0))
        ep = jnp.where(later, ep * sib, ep)
        ex = jnp.where(later, ex, ex * sib)
        et = et * sib
    scores = jnp.where(pair_code < 0, 0.0, scores)

    st = st_ref[...]
    o = _dot_nt((q * ep).astype(BF16), st.astype(BF16)) + _dot(scores.astype(BF16), vb)
    st_ref[...] = et[0:1, :] * st + _dot_tn(vb, (kk * ex).astype(BF16))
    return o


def _hgrn_kernel(q_ref, zf_ref, zb_ref, v_ref, g_ref, lb_ref, nw_ref, o_ref,
                 of_scr, ob_scr, stf_scr, stb_scr, *, chunk):
    s_len = q_ref.shape[0]
    n_chunks = s_len // chunk
    row = lax.broadcasted_iota(jnp.int32, (chunk, HEAD_DIM), 0)
    ri = lax.broadcasted_iota(jnp.int32, (chunk, chunk), 0)
    ci = lax.broadcasted_iota(jnp.int32, (chunk, chunk), 1)
    code_f = jnp.where(ri >= ci, ri ^ ci, -1)
    code_b = jnp.where(ci >= ri, ri ^ ci, -1)
    n_heads = q_ref.shape[1] // HEAD_DIM
    stf_scr[...] = jnp.zeros_like(stf_scr)
    stb_scr[...] = jnp.zeros_like(stb_scr)

    def body(i, carry):
        rf = pl.multiple_of(i * chunk, chunk)
        rb = pl.multiple_of((n_chunks - 1 - i) * chunk, chunk)
        for hd in range(n_heads):
            sl = slice(hd * HEAD_DIM, (hd + 1) * HEAD_DIM)
            of_scr[pl.ds(rf, chunk), sl] = _hgrn_chunk(
                q_ref[pl.ds(rf, chunk), sl], zf_ref[pl.ds(rf, chunk), sl], v_ref[pl.ds(rf, chunk), sl],
                lb_ref[0:1, sl], stf_scr.at[hd], False, row, code_f)
            ob_scr[pl.ds(rb, chunk), sl] = _hgrn_chunk(
                q_ref[pl.ds(rb, chunk), sl], zb_ref[pl.ds(rb, chunk), sl], v_ref[pl.ds(rb, chunk), sl],
                lb_ref[1:2, sl], stb_scr.at[hd], True, row, code_b)
        return carry

    lax.fori_loop(0, n_chunks, body, 0)

    def epilogue(i, carry):
        r = pl.multiple_of(i * chunk, chunk)
        for hd in range(n_heads):
            sl = slice(hd * HEAD_DIM, (hd + 1) * HEAD_DIM)
            o = of_scr[pl.ds(r, chunk), sl] + ob_scr[pl.ds(r, chunk), sl]
            y = o * lax.rsqrt(jnp.mean(o * o, axis=-1, keepdims=True) + RMS_EPS) * nw_ref[:, sl]
            g = g_ref[pl.ds(r, chunk), sl]
            o_ref[pl.ds(r, chunk), sl] = (y * (g * _sigmoid(g))).astype(o_ref.dtype)
        return carry

    lax.fori_loop(0, n_chunks, epilogue, 0)


def _hgrn2(u3, lb, norm_w, a_w):
    bsz, s_len, _ = u3.shape
    heads = a_w // HEAD_DIM
    chunk = min(HG_CHUNK, s_len)
    hps = HG_HEADS_PER_STEP
    wid = hps * HEAD_DIM
    groups = heads // hps

    def col(k):
        return pl.BlockSpec((None, s_len, wid), lambda b, h: (b, 0, k * groups + h))

    out = pl.pallas_call(
        functools.partial(_hgrn_kernel, chunk=chunk),
        grid=(bsz, groups),
        in_specs=[col(0), col(1), col(2), col(3), col(4),
                  pl.BlockSpec((2, wid), lambda b, h: (0, h)),
                  pl.BlockSpec((1, wid), lambda b, h: (0, h))],
        out_specs=pl.BlockSpec((None, s_len, wid), lambda b, h: (b, 0, h)),
        out_shape=jax.ShapeDtypeStruct((bsz, s_len, a_w), BF16),
        scratch_shapes=[pltpu.VMEM((s_len, wid), F32), pltpu.VMEM((s_len, wid), F32),
                        pltpu.VMEM((hps, HEAD_DIM, HEAD_DIM), F32), pltpu.VMEM((hps, HEAD_DIM, HEAD_DIM), F32)],
        compiler_params=_params(("parallel", "arbitrary"), 14 * s_len * wid * 4 + (16 << 20)),
        name="hgrn2",
    )(u3, u3, u3, u3, u3, lb, norm_w.reshape(1, a_w))
    return out.reshape(bsz * s_len, a_w)


def _rglru_gate_kernel(x_ref, xp_ref, xn_ref, cw_ref, cb_ref, w_ref, bias_ref, lam_ref,
                       af_ref, bf_ref, ab_ref, bb_ref):
    ts, width = x_ref.shape
    i = pl.program_id(1)
    has_prev = (i > 0).astype(F32)
    has_next = (i < pl.num_programs(1) - 1).astype(F32)
    x = x_ref[...]
    row = lax.broadcasted_iota(jnp.int32, (ts, width), 0)
    p6 = xp_ref[SUBLANES - 2:SUBLANES - 1, :] * has_prev
    p7 = xp_ref[SUBLANES - 1:SUBLANES, :] * has_prev
    n0 = xn_ref[0:1, :] * has_next
    xm1 = jnp.where(row == 0, p7, pltpu.roll(x, 1, 0))
    xm2 = jnp.where(row == 0, p6, jnp.where(row == 1, p7, pltpu.roll(x, 2, 0)))
    xp1 = jnp.where(row == ts - 1, n0, pltpu.roll(x, ts - 1, 0))
    xc = (cb_ref[...] + cw_ref[0:1, :] * xm2 + cw_ref[1:2, :] * xm1
          + cw_ref[2:3, :] * x + cw_ref[3:4, :] * xp1)
    lam = lam_ref[...]
    sp = jnp.maximum(-lam, 0.0) + jnp.log1p(jnp.exp(-jnp.abs(lam)))
    outs = ((af_ref, bf_ref), (ab_ref, bb_ref))
    for n in range(width // HEAD_DIM):
        sl = slice(n * HEAD_DIM, (n + 1) * HEAD_DIM)
        xn_blk = xc[:, sl]
        zz = _dot(xn_blk.astype(BF16), w_ref[n])
        for d in range(2):
            r = _sigmoid(zz[:, (2 * d) * HEAD_DIM:(2 * d + 1) * HEAD_DIM] + bias_ref[2 * d:2 * d + 1, sl])
            ig = _sigmoid(zz[:, (2 * d + 1) * HEAD_DIM:(2 * d + 2) * HEAD_DIM]
                          + bias_ref[2 * d + 1:2 * d + 2, sl])
            log_a = (-RG_C) * r * sp[d:d + 1, sl]
            a = jnp.exp(log_a)
            mult = jnp.sqrt(-jnp.tanh(log_a) * (a * a + 1.0))
            outs[d][0][:, sl] = a
            outs[d][1][:, sl] = mult * (ig * xn_blk)


def _rglru_gates(u3, col0, conv_w, conv_b, w_packed, bias, lam, b_w):
    bsz, s_len, _ = u3.shape
    ts = min(RG_ROWS, s_len)
    nt = s_len // ts
    cb = col0 // b_w
    rows8 = ts // SUBLANES
    out_sd = jax.ShapeDtypeStruct((s_len, bsz * b_w), F32)
    out_spec = pl.BlockSpec((ts, b_w), lambda b, i: (i, b))
    return pl.pallas_call(
        _rglru_gate_kernel,
        grid=(bsz, nt),
        in_specs=[pl.BlockSpec((None, ts, b_w), lambda b, i: (b, i, cb)),
                  pl.BlockSpec((None, SUBLANES, b_w), lambda b, i: (b, jnp.maximum(i * rows8 - 1, 0), cb)),
                  pl.BlockSpec((None, SUBLANES, b_w),
                               lambda b, i: (b, jnp.minimum((i + 1) * rows8, s_len // SUBLANES - 1), cb)),
                  pl.BlockSpec((4, b_w), lambda b, i: (0, 0)),
                  pl.BlockSpec((1, b_w), lambda b, i: (0, 0)),
                  pl.BlockSpec(w_packed.shape, lambda b, i: (0, 0, 0)),
                  pl.BlockSpec((4, b_w), lambda b, i: (0, 0)),
                  pl.BlockSpec((2, b_w), lambda b, i: (0, 0))],
        out_specs=[out_spec, out_spec, out_spec, out_spec],
        out_shape=[out_sd, out_sd, out_sd, out_sd],
        compiler_params=_params(("parallel", "arbitrary"), 2 * 5 * ts * b_w * 4 + (16 << 20)),
        name="rglru_gates",
    )(u3, u3, u3, conv_w, conv_b.reshape(1, b_w), w_packed, bias, lam)


def _rglru_scan_kernel(af_ref, bf_ref, ab_ref, bb_ref, hf_ref, hb_ref, cf_scr, cb_scr):
    ts = af_ref.shape[0]

    @pl.when(pl.program_id(1) == 0)
    def _():
        cf_scr[...] = jnp.zeros_like(cf_scr)
        cb_scr[...] = jnp.zeros_like(cb_scr)

    def body(t, carry):
        hf, hb = carry
        hf = af_ref[t] * hf + bf_ref[t]
        hf_ref[t] = hf
        tb = ts - 1 - t
        hb = ab_ref[tb] * hb + bb_ref[tb]
        hb_ref[tb] = hb
        return hf, hb

    hf, hb = lax.fori_loop(0, ts, body, (cf_scr[...], cb_scr[...]), unroll=8)
    cf_scr[...] = hf
    cb_scr[...] = hb


def _rglru_scan(af, bf, ab, bb, bsz, b_w):
    s_len = af.shape[0]
    ts = min(SCAN_ROWS, s_len)
    tl = min(SCAN_LANES, b_w)
    nt = s_len // ts
    shape3 = (s_len, bsz, b_w)
    fwd = pl.BlockSpec((ts, bsz, tl), lambda l, i: (i, 0, l))
    bwd = pl.BlockSpec((ts, bsz, tl), lambda l, i: (nt - 1 - i, 0, l))
    sd = jax.ShapeDtypeStruct(shape3, F32)
    hf, hb = pl.pallas_call(
        _rglru_scan_kernel,
        grid=(b_w // tl, nt),
        in_specs=[fwd, fwd, bwd, bwd],
        out_specs=[fwd, bwd],
        out_shape=[sd, sd],
        scratch_shapes=[pltpu.VMEM((bsz, tl), F32), pltpu.VMEM((bsz, tl), F32)],
        compiler_params=_params(("parallel", "arbitrary"), 2 * 6 * ts * bsz * tl * 4 + (8 << 20)),
        name="rglru_scan",
    )(af.reshape(shape3), bf.reshape(shape3), ab.reshape(shape3), bb.reshape(shape3))
    return hf.reshape(s_len, bsz * b_w), hb.reshape(s_len, bsz * b_w)


def _attn_prep_kernel(q0_ref, q1_ref, kv_ref, cos_ref, sa_ref, sb_ref, qn_ref, kn_ref,
                      qo_ref, ko_ref, vo_ref):
    cos = cos_ref[...]
    sin_a = sa_ref[...]
    sin_b = sb_ref[...]

    def norm_rope(t, w):
        y = t * lax.rsqrt(jnp.mean(t * t, axis=-1, keepdims=True) + RMS_EPS) * w
        return (y * cos + pltpu.roll(y, HEAD_DIM - ROPE_HALF // 2, 1) * sin_a
                + pltpu.roll(y, ROPE_HALF // 2, 1) * sin_b)

    half_heads = q0_ref.shape[1] // HEAD_DIM
    kv_heads = ko_ref.shape[1] // HEAD_DIM
    qn = qn_ref[...]
    kn = kn_ref[...]
    for hd in range(half_heads):
        sl = slice(hd * HEAD_DIM, (hd + 1) * HEAD_DIM)
        so = slice((half_heads + hd) * HEAD_DIM, (half_heads + hd + 1) * HEAD_DIM)
        qo_ref[:, sl] = norm_rope(q0_ref[:, sl], qn).astype(BF16)
        qo_ref[:, so] = norm_rope(q1_ref[:, sl], qn).astype(BF16)
    for hd in range(kv_heads):
        sl = slice(hd * HEAD_DIM, (hd + 1) * HEAD_DIM)
        ko_ref[:, sl] = norm_rope(kv_ref[:, sl], kn).astype(BF16)
    vo_ref[...] = kv_ref[:, kv_heads * HEAD_DIM:].T.astype(BF16)


def _attn_prep(u2, col0, c_w, kv_w, s_len, cos, sin_a, sin_b, qn_w, kn_w):
    t = u2.shape[0]
    tr = min(PREP_ROWS, s_len)
    half = c_w // 2
    assert col0 % half == 0 and 2 * kv_w == half
    cb = col0 // half
    npos = s_len // tr
    tab = pl.BlockSpec((tr, HEAD_DIM), lambda i: (i % npos, 0))
    vec = pl.BlockSpec((1, HEAD_DIM), lambda i: (0, 0))
    return pl.pallas_call(
        _attn_prep_kernel,
        grid=(t // tr,),
        in_specs=[pl.BlockSpec((tr, half), lambda i: (i, cb)),
                  pl.BlockSpec((tr, half), lambda i: (i, cb + 1)),
                  pl.BlockSpec((tr, half), lambda i: (i, cb + 2)),
                  tab, tab, tab, vec, vec],
        out_specs=[pl.BlockSpec((tr, c_w), lambda i: (i, 0)),
                   pl.BlockSpec((tr, kv_w), lambda i: (i, 0)),
                   pl.BlockSpec((kv_w, tr), lambda i: (0, i))],
        out_shape=[jax.ShapeDtypeStruct((t, c_w), BF16),
                   jax.ShapeDtypeStruct((t, kv_w), BF16),
                   jax.ShapeDtypeStruct((kv_w, t), BF16)],
        compiler_params=_params(("parallel",), 2 * tr * (3 * half * 4 + (c_w + 2 * kv_w) * 2) + (16 << 20)),
        name="attn_prep",
    )(u2, u2, u2, cos, sin_a, sin_b, qn_w.reshape(1, HEAD_DIM), kn_w.reshape(1, HEAD_DIM))


def _attn_kernel(q_ref, k_ref, vt_ref, o_ref, sc_scr):
    k = k_ref[...]
    vt = vt_ref[...]
    vt_aug = jnp.concatenate([vt, jnp.ones((BF16_ROWS, vt.shape[1]), BF16)], axis=0)
    s_len = k.shape[0]
    kc = min(ATTN_KB, s_len)
    n_kc = s_len // kc

    def scores(g, c, run_max):
        rows = slice(c * kc, (c + 1) * kc)
        sc = _dot_nt(k[rows, :], q_ref[:, g * HEAD_DIM:(g + 1) * HEAD_DIM])
        sc_scr[g % 2, rows, :] = sc
        cmax = jnp.max(sc.reshape(kc // SUBLANES, SUBLANES, sc.shape[1]), axis=0)
        return cmax if run_max is None else jnp.maximum(run_max, cmax)

    next_max = None
    for c in range(n_kc):
        next_max = scores(0, c, next_max)
    for g in range(C_GROUP):
        mx = jnp.max(next_max, axis=0, keepdims=True)
        next_max = None
        acc = None
        for c in range(n_kc):
            rows = slice(c * kc, (c + 1) * kc)
            p = jnp.exp2((sc_scr[g % 2, rows, :] - mx) * (ATTN_SCALE * LOG2_E)).astype(BF16)
            if g + 1 < C_GROUP:
                next_max = scores(g + 1, c, next_max)
            part = _dot(vt_aug[:, rows], p)
            acc = part if acc is None else acc + part
        ot = acc[:HEAD_DIM, :] / acc[HEAD_DIM:HEAD_DIM + 1, :]
        o_ref[:, g * HEAD_DIM:(g + 1) * HEAD_DIM] = ot.T.astype(o_ref.dtype)


def _attention(qr, kr, vt, bsz, s_len):
    t, c_w = qr.shape
    kv_heads = kr.shape[1] // HEAD_DIM
    gw = C_GROUP * HEAD_DIM
    tq = min(ATTN_TQ, s_len)
    nq = s_len // tq
    return pl.pallas_call(
        _attn_kernel,
        grid=(bsz, kv_heads, nq),
        in_specs=[pl.BlockSpec((tq, gw), lambda b, h, i: (b * nq + i, h)),
                  pl.BlockSpec((s_len, HEAD_DIM), lambda b, h, i: (b, h)),
                  pl.BlockSpec((HEAD_DIM, s_len), lambda b, h, i: (h, b))],
        out_specs=pl.BlockSpec((tq, gw), lambda b, h, i: (b * nq + i, h)),
        out_shape=jax.ShapeDtypeStruct((t, c_w), BF16),
        scratch_shapes=[pltpu.VMEM((2, s_len, tq), F32)],
        compiler_params=_params(("parallel", "parallel", "arbitrary"),
                                8 * tq * s_len * 4 + 4 * s_len * HEAD_DIM * 2 + (8 << 20)),
        name="attention",
    )(qr, kr, vt)


def _gelu_tanh(x):
    return 0.5 * x * (1.0 + jnp.tanh(math.sqrt(2.0 / math.pi) * (x + 0.044715 * (x * x * x))))


def _outproj_kernel(ya_ref, hf_ref, hb_ref, gate_ref, yc_ref, wa_ref, wb_ref, wc_ref, res_ref,
                    o_ref, yb_scr, *, alpha):
    @pl.when(pl.program_id(1) == 0)
    def _():
        yb_scr[...] = (_gelu_tanh(gate_ref[...]) * (hf_ref[...] + hb_ref[...])).astype(BF16)

    acc = _dot(ya_ref[...], wa_ref[...])
    acc = acc + _dot(yb_scr[...], wb_ref[...])
    acc = acc + _dot(yc_ref[...], wc_ref[...])
    o_ref[...] = alpha * res_ref[...] + acc


def _outproj(ya, hf, hb, u2, gate_col0, yc, w_out, layer, res, alpha, s_len):
    t, a_w = ya.shape
    b_w = hf.shape[1] // (t // s_len)
    c_w = yc.shape[1]
    n = w_out.shape[2]
    assert a_w == b_w and c_w % a_w == 0 and gate_col0 % b_w == 0
    tm = min(OUT_TM, s_len)
    tn = min(MM_TN, n)
    ns = s_len // tm
    gcb = gate_col0 // b_w
    c_blk = (a_w + b_w) // c_w
    assert c_blk * c_w == a_w + b_w
    vm = 2 * (tm * a_w * 2 + 3 * tm * b_w * 4 + tm * c_w * 2 + (a_w + b_w + c_w) * tn * 2
              + 2 * tm * tn * 4) + tm * b_w * 2 + tm * tn * 4 + (4 << 20)
    return pl.pallas_call(
        functools.partial(_outproj_kernel, alpha=alpha),
        grid=(t // tm, n // tn),
        in_specs=[pl.BlockSpec((tm, a_w), lambda i, j: (i, 0)),
                  pl.BlockSpec((tm, b_w), lambda i, j: (i % ns, i // ns)),
                  pl.BlockSpec((tm, b_w), lambda i, j: (i % ns, i // ns)),
                  pl.BlockSpec((tm, b_w), lambda i, j: (i, gcb)),
                  pl.BlockSpec((tm, c_w), lambda i, j: (i, 0)),
                  pl.BlockSpec((None, a_w, tn), lambda i, j: (layer, 0, j)),
                  pl.BlockSpec((None, b_w, tn), lambda i, j: (layer, 1, j)),
                  pl.BlockSpec((None, c_w, tn), lambda i, j: (layer, c_blk, j)),
                  pl.BlockSpec((tm, tn), lambda i, j: (i, j))],
        out_specs=pl.BlockSpec((tm, tn), lambda i, j: (i, j)),
        out_shape=jax.ShapeDtypeStruct((t, n), F32),
        scratch_shapes=[pltpu.VMEM((tm, b_w), BF16)],
        compiler_params=_params(("parallel", "arbitrary"), vm),
        name="outproj",
    )(ya, hf, hb, u2, yc, w_out, w_out, w_out, res)


def _ffn_up_kernel(x_ref, w_ref, gp_ref, gn_ref, cw_ref, cb_ref, o_ref, *, tiles_per_seq):
    tm, tn = o_ref.shape
    gu = _dot(x_ref[...], w_ref[...])
    g = gu[:, :tn]
    up = gu[:, tn:]
    pos = pl.program_id(0) % tiles_per_seq
    has_prev = (pos > 0).astype(F32)
    has_next = (pos < tiles_per_seq - 1).astype(F32)
    g_prev = gp_ref[0, 1:2, :tn] * has_prev
    g_next = gn_ref[0, 0:1, :tn] * has_next
    row = lax.broadcasted_iota(jnp.int32, (tm, tn), 0)
    g_m1 = jnp.where(row == 0, g_prev, pltpu.roll(g, 1, 0))
    g_p1 = jnp.where(row == tm - 1, g_next, pltpu.roll(g, tm - 1, 0))
    y = cb_ref[...] + cw_ref[0:1, :] * g_m1 + cw_ref[1:2, :] * g + cw_ref[2:3, :] * g_p1
    o_ref[...] = ((y * _sigmoid(y)) * up).astype(o_ref.dtype)


def _interleave_gate_up(w_up, tn):
    n_l, d, two_ff = w_up.shape
    nj = two_ff // (2 * tn)
    return w_up.reshape(n_l, d, 2, nj, tn).transpose(0, 1, 3, 2, 4).reshape(n_l, d, two_ff)


def _ffn_up(hbf, w_il, layer, conv_w, conv_b, s_len):
    t, d = hbf.shape
    d_ff = w_il.shape[2] // 2
    tm = min(FFN_TM, s_len)
    tn = FFN_TN
    assert d_ff % tn == 0
    nj = d_ff // tn
    n_tiles = t // tm
    tiles_per_seq = s_len // tm
    edge_x = hbf.reshape(n_tiles, tm, d)[:, (0, tm - 1), :].reshape(2 * n_tiles, d)
    g_edge = _matmul(edge_x, w_il, layer, 2 * n_tiles, 2 * tn, F32, "ffn_edge").reshape(n_tiles, 2, 2 * d_ff)
    vm = 2 * (tm * d * 2 + 2 * d * tn * 2 + tm * tn * 2) + 8 * tm * tn * 4 + (4 << 20)
    return pl.pallas_call(
        functools.partial(_ffn_up_kernel, tiles_per_seq=tiles_per_seq),
        grid=(n_tiles, nj),
        in_specs=[pl.BlockSpec((tm, d), lambda i, j: (i, 0)),
                  pl.BlockSpec((None, d, 2 * tn), lambda i, j: (layer, 0, j)),
                  pl.BlockSpec((1, 2, 2 * tn), lambda i, j: (jnp.maximum(i - 1, 0), 0, j)),
                  pl.BlockSpec((1, 2, 2 * tn), lambda i, j: (jnp.minimum(i + 1, n_tiles - 1), 0, j)),
                  pl.BlockSpec((3, tn), lambda i, j: (0, j)),
                  pl.BlockSpec((1, tn), lambda i, j: (0, j))],
        out_specs=pl.BlockSpec((tm, tn), lambda i, j: (i, j)),
        out_shape=jax.ShapeDtypeStruct((t, d_ff), BF16),
        compiler_params=_params(("parallel", "arbitrary"), vm),
        name="ffn_up",
    )(hbf, w_il, g_edge, g_edge, conv_w, conv_b.reshape(1, d_ff))


def _rope_tables(s_len):
    rows = s_len // GRID_W
    g_r, g_c = jnp.meshgrid(jnp.arange(rows), jnp.arange(GRID_W), indexing='ij')
    row = g_r.reshape(s_len).astype(F32)
    colp = g_c.reshape(s_len).astype(F32)
    inv_freq = ROPE_THETA ** (-jnp.arange(0, ROPE_HALF, 2, dtype=F32) / ROPE_HALF)
    ang_r = row[:, None] * inv_freq[None, :]
    ang_c = colp[:, None] * inv_freq[None, :]
    ang = jnp.concatenate([ang_r, ang_r, ang_c, ang_c], axis=-1)
    cos, sin = jnp.cos(ang), jnp.sin(ang)
    first_quarter = (jnp.arange(HEAD_DIM) & (ROPE_HALF // 2)) == 0
    sin_a = jnp.where(first_quarter, -sin, 0.0)
    sin_b = jnp.where(first_quarter, 0.0, sin)
    return cos, sin_a, sin_b


def kernel(x, emb_ln_w, emb_ln_b, w_in, hgrn_lb_logits, hgrn_norm_w, rglru_conv_w, rglru_conv_b,
           rglru_wa, rglru_ba, rglru_wx, rglru_bx, rglru_lambda, attn_q_norm_w, attn_k_norm_w,
           w_out, ln1_w, ln1_b, ffn_w_up, ffn_conv_w, ffn_conv_b, ffn_w_down, ln2_w, ln2_b):
    bsz, s_len, d_model = x.shape
    depth = w_in.shape[0]
    t = bsz * s_len
    a_w = hgrn_norm_w.shape[1]
    b_w = rglru_conv_w.shape[2]
    in_cols = w_in.shape[2]
    a_cols = 5 * a_w
    b_cols = 2 * b_w
    c_w = w_out.shape[1] - a_w - b_w
    kv_w = (in_cols - a_cols - b_cols - c_w) // 2
    alpha = (2.0 * depth) ** 0.25

    cos, sin_a, sin_b = _rope_tables(s_len)
    lb_cs = jnp.cumsum(jax.nn.softmax(hgrn_lb_logits.astype(F32), axis=0), axis=0)
    lower_bounds = lb_cs - lb_cs[0:1]

    w_in_bf = w_in.astype(BF16)
    w_out_bf = w_out.astype(BF16)
    w_up_bf = _interleave_gate_up(ffn_w_up.astype(BF16), FFN_TN)
    w_down_bf = ffn_w_down.astype(BF16)

    h, hbf = _layernorm(x.reshape(t, d_model), emb_ln_w, emb_ln_b)
    for l in range(depth):
        u2 = _matmul(hbf, w_in_bf, l, MM_TM, MM_TN, F32, "in_proj")
        u3 = u2.reshape(bsz, s_len, in_cols)

        ya = _hgrn2(u3, lower_bounds[l], hgrn_norm_w[l], a_w)

        nb = b_w // HEAD_DIM
        w_packed = jnp.concatenate([rglru_wa[l, 0], rglru_wx[l, 0], rglru_wa[l, 1], rglru_wx[l, 1]],
                                   axis=-1).astype(BF16)
        assert w_packed.shape == (nb, HEAD_DIM, 4 * HEAD_DIM)
        bias = jnp.stack([rglru_ba[l, 0], rglru_bx[l, 0], rglru_ba[l, 1], rglru_bx[l, 1]])
        af, bf, ab, bb = _rglru_gates(u3, a_cols, rglru_conv_w[l], rglru_conv_b[l], w_packed, bias,
                                      rglru_lambda[l], b_w)
        hf, hb = _rglru_scan(af, bf, ab, bb, bsz, b_w)

        qr, kr, vt = _attn_prep(u2, a_cols + b_cols, c_w, kv_w, s_len, cos, sin_a, sin_b,
                                attn_q_norm_w[l], attn_k_norm_w[l])
        yc = _attention(qr, kr, vt, bsz, s_len)

        pre = _outproj(ya, hf, hb, u2, a_cols + b_w, yc, w_out_bf, l, h, alpha, s_len)
        h, hbf = _layernorm(pre, ln1_w[l], ln1_b[l])

        act = _ffn_up(hbf, w_up_bf, l, ffn_conv_w[l], ffn_conv_b[l], s_len)
        pre = _matmul_residual(act, w_down_bf, l, h, alpha, DOWN_TM, DOWN_TN, "ffn_down")
        h, hbf = _layernorm(pre, ln2_w[l], ln2_b[l])
    return h.reshape(bsz, s_len, d_model)
```

```python
import functools
import math

import jax
import jax.numpy as jnp
from jax import lax
from jax.experimental import pallas as pl
from jax.experimental.pallas import tpu as pltpu

F32 = jnp.float32
BF16 = jnp.bfloat16

HEAD_DIM = 128
RG_C = 8.0
C_GROUP = 4
ROPE_THETA = 10000.0
ROPE_HALF = HEAD_DIM // 2
GRID_W = 64
LN_EPS = 1e-5
RMS_EPS = 1e-6
ATTN_SCALE = HEAD_DIM ** -0.5
LOG2_E = math.log2(math.e)

V7X_VMEM_BYTES = 64 * 1024 * 1024
LANES = 128
SUBLANES = 8
BF16_ROWS = 16

HG_CHUNK = 128
HG_HEADS_PER_STEP = 2
LN_ROWS = 256
MM_TM = 1024
MM_TN = 1024
OUT_TM = 1024
FFN_TM = 1024
FFN_TN = 256
DOWN_TM = 512
DOWN_TN = 512
ATTN_TQ = 512
ATTN_KB = 256
PREP_ROWS = 512
RG_ROWS = 512


def _vmem_limit(nbytes):
    return int(min(V7X_VMEM_BYTES - 4 * 1024 * 1024, max(nbytes, 16 * 1024 * 1024)))


def _params(sem, vmem_bytes):
    return pltpu.CompilerParams(dimension_semantics=sem, vmem_limit_bytes=_vmem_limit(vmem_bytes))


def _dot(a, b):
    return jnp.dot(a, b, preferred_element_type=F32)


def _dot_nt(a, b):
    return lax.dot_general(a, b, (((1,), (1,)), ((), ())), preferred_element_type=F32)


def _dot_tn(a, b):
    return lax.dot_general(a, b, (((0,), (0,)), ((), ())), preferred_element_type=F32)


def _sigmoid(x):
    return 1.0 / (1.0 + jnp.exp(-x))


def _ln_kernel(x_ref, w_ref, b_ref, o_ref, obf_ref):
    x = x_ref[...]
    mu = jnp.mean(x, axis=-1, keepdims=True)
    xc = x - mu
    var = jnp.mean(xc * xc, axis=-1, keepdims=True)
    y = xc * lax.rsqrt(var + LN_EPS) * w_ref[...] + b_ref[...]
    o_ref[...] = y
    obf_ref[...] = y.astype(BF16)


def _layernorm(x, w, b):
    t, d = x.shape
    tr = min(LN_ROWS, t)
    return pl.pallas_call(
        _ln_kernel,
        grid=(t // tr,),
        in_specs=[pl.BlockSpec((tr, d), lambda i: (i, 0)),
                  pl.BlockSpec((1, d), lambda i: (0, 0)),
                  pl.BlockSpec((1, d), lambda i: (0, 0))],
        out_specs=[pl.BlockSpec((tr, d), lambda i: (i, 0)),
                   pl.BlockSpec((tr, d), lambda i: (i, 0))],
        out_shape=[jax.ShapeDtypeStruct((t, d), F32), jax.ShapeDtypeStruct((t, d), BF16)],
        compiler_params=_params(("parallel",), 2 * tr * d * 10 + (8 << 20)),
        name="layernorm",
    )(x, w.reshape(1, d), b.reshape(1, d))


def _mm_kernel(x_ref, w_ref, o_ref):
    o_ref[...] = _dot(x_ref[...], w_ref[...]).astype(o_ref.dtype)


def _matmul(x, w, layer, tm, tn, out_dtype, name, n_cols=None):
    m, k = x.shape
    n = w.shape[2] if n_cols is None else n_cols
    tm, tn = min(tm, m), min(tn, n)
    vm = 2 * (tm * k * 2 + k * tn * 2 + tm * tn * 4) + tm * tn * 4 + (4 << 20)
    return pl.pallas_call(
        _mm_kernel,
        grid=(m // tm, n // tn),
        in_specs=[pl.BlockSpec((tm, k), lambda i, j: (i, 0)),
                  pl.BlockSpec((None, k, tn), lambda i, j: (layer, 0, j))],
        out_specs=pl.BlockSpec((tm, tn), lambda i, j: (i, j)),
        out_shape=jax.ShapeDtypeStruct((m, n), out_dtype),
        compiler_params=_params(("parallel", "arbitrary"), vm),
        name=name,
    )(x, w)


def _mm_res_kernel(x_ref, w_ref, r_ref, o_ref, *, alpha):
    o_ref[...] = alpha * r_ref[...] + _dot(x_ref[...], w_ref[...])


def _matmul_residual(x, w, layer, res, alpha, tm, tn, name):
    m, k = x.shape
    n = w.shape[2]
    tm, tn = min(tm, m), min(tn, n)
    vm = 2 * (tm * k * 2 + k * tn * 2 + 2 * tm * tn * 4) + tm * tn * 4 + (4 << 20)
    return pl.pallas_call(
        functools.partial(_mm_res_kernel, alpha=alpha),
        grid=(m // tm, n // tn),
        in_specs=[pl.BlockSpec((tm, k), lambda i, j: (i, 0)),
                  pl.BlockSpec((None, k, tn), lambda i, j: (layer, 0, j)),
                  pl.BlockSpec((tm, tn), lambda i, j: (i, j))],
        out_specs=pl.BlockSpec((tm, tn), lambda i, j: (i, j)),
        out_shape=jax.ShapeDtypeStruct((m, n), F32),
        compiler_params=_params(("parallel", "arbitrary"), vm),
        name=name,
    )(x, w, res)


def _hgrn_chunk(q, z, v, lb, st_ref, rev, row, pair_code):
    c_len = q.shape[0]
    n_lvl = c_len.bit_length() - 1
    t = jnp.exp(-jnp.abs(z))
    r = 1.0 / (1.0 + t)
    tr = t * r
    pos = z >= 0.0
    f = lb + (1.0 - lb) * jnp.where(pos, r, tr)
    kk = (1.0 - lb) * jnp.where(pos, tr, r)

    vb = v.astype(BF16)
    scores = _dot_nt(q.astype(BF16), kk.astype(BF16))
    ep = f
    ex = jnp.ones_like(f)
    et = f
    for lvl in range(n_lvl):
        m = 1 << lvl
        later = ((row & m) == 0) if rev else ((row & m) != 0)
        x = jnp.where(later, q * ep, kk * ex).astype(BF16)
        scores = jnp.where(pair_code >= m, _dot_nt(x, x), scores)
        from_earlier, from_later = (c_len - m, m) if rev else (m, c_len - m)
        sib = jnp.where(later, pltpu.roll(et, from_earlier, 0), pltpu.roll(et, from_later, 0))
        ep = jnp.where(later, ep * sib, ep)
        ex = jnp.where(later, ex, ex * sib)
        et = et * sib
    scores = jnp.where(pair_code < 0, 0.0, scores)

    st = st_ref[...]
    o = _dot_nt((q * ep).astype(BF16), st.astype(BF16)) + _dot(scores.astype(BF16), vb)
    st_ref[...] = et[0:1, :] * st + _dot_tn(vb, (kk * ex).astype(BF16))
    return o


def _hgrn_kernel(q_ref, zf_ref, zb_ref, v_ref, g_ref, lb_ref, nw_ref, o_ref,
                 of_scr, ob_scr, stf_scr, stb_scr, *, chunk):
    s_len = q_ref.shape[0]
    n_chunks = s_len // chunk
    row = lax.broadcasted_iota(jnp.int32, (chunk, HEAD_DIM), 0)
    ri = lax.broadcasted_iota(jnp.int32, (chunk, chunk), 0)
    ci = lax.broadcasted_iota(jnp.int32, (chunk, chunk), 1)
    code_f = jnp.where(ri >= ci, ri ^ ci, -1)
    code_b = jnp.where(ci >= ri, ri ^ ci, -1)
    n_heads = q_ref.shape[1] // HEAD_DIM
    stf_scr[...] = jnp.zeros_like(stf_scr)
    stb_scr[...] = jnp.zeros_like(stb_scr)

    def body(i, carry):
        rf = pl.multiple_of(i * chunk, chunk)
        rb = pl.multiple_of((n_chunks - 1 - i) * chunk, chunk)
        for hd in range(n_heads):
            sl = slice(hd * HEAD_DIM, (hd + 1) * HEAD_DIM)
            of_scr[pl.ds(rf, chunk), sl] = _hgrn_chunk(
                q_ref[pl.ds(rf, chunk), sl], zf_ref[pl.ds(rf, chunk), sl], v_ref[pl.ds(rf, chunk), sl],
                lb_ref[0:1, sl], stf_scr.at[hd], False, row, code_f)
            ob_scr[pl.ds(rb, chunk), sl] = _hgrn_chunk(
                q_ref[pl.ds(rb, chunk), sl], zb_ref[pl.ds(rb, chunk), sl], v_ref[pl.ds(rb, chunk), sl],
                lb_ref[1:2, sl], stb_scr.at[hd], True, row, code_b)
        return carry

    lax.fori_loop(0, n_chunks, body, 0)

    def epilogue(i, carry):
        r = pl.multiple_of(i * chunk, chunk)
        for hd in range(n_heads):
            sl = slice(hd * HEAD_DIM, (hd + 1) * HEAD_DIM)
            o = of_scr[pl.ds(r, chunk), sl] + ob_scr[pl.ds(r, chunk), sl]
            y = o * lax.rsqrt(jnp.mean(o * o, axis=-1, keepdims=True) + RMS_EPS) * nw_ref[:, sl]
            g = g_ref[pl.ds(r, chunk), sl]
            o_ref[pl.ds(r, chunk), sl] = (y * (g * _sigmoid(g))).astype(o_ref.dtype)
        return carry

    lax.fori_loop(0, n_chunks, epilogue, 0)


def _hgrn2(u3, lb, norm_w, a_w):
    bsz, s_len, _ = u3.shape
    heads = a_w // HEAD_DIM
    chunk = min(HG_CHUNK, s_len)
    hps = HG_HEADS_PER_STEP
    wid = hps * HEAD_DIM
    groups = heads // hps

    def col(k):
        return pl.BlockSpec((None, s_len, wid), lambda b, h: (b, 0, k * groups + h))

    out = pl.pallas_call(
        functools.partial(_hgrn_kernel, chunk=chunk),
        grid=(bsz, groups),
        in_specs=[col(0), col(1), col(2), col(3), col(4),
                  pl.BlockSpec((2, wid), lambda b, h: (0, h)),
                  pl.BlockSpec((1, wid), lambda b, h: (0, h))],
        out_specs=pl.BlockSpec((None, s_len, wid), lambda b, h: (b, 0, h)),
        out_shape=jax.ShapeDtypeStruct((bsz, s_len, a_w), BF16),
        scratch_shapes=[pltpu.VMEM((s_len, wid), F32), pltpu.VMEM((s_len, wid), F32),
                        pltpu.VMEM((hps, HEAD_DIM, HEAD_DIM), F32), pltpu.VMEM((hps, HEAD_DIM, HEAD_DIM), F32)],
        compiler_params=_params(("parallel", "arbitrary"), 14 * s_len * wid * 4 + (16 << 20)),
        name="hgrn2",
    )(u3, u3, u3, u3, u3, lb, norm_w.reshape(1, a_w))
    return out.reshape(bsz * s_len, a_w)


def _gelu_tanh(x):
    return 0.5 * x * (1.0 + jnp.tanh(math.sqrt(2.0 / math.pi) * (x + 0.044715 * (x * x * x))))


def _rglru_kernel(x_ref, xp_ref, xn_ref, cw_ref, cb_ref, w_ref, bias_ref, lam_ref, *rest, rev):
    if rev:
        gate_ref, hf_ref, o_ref, carry_scr = rest
    else:
        o_ref, carry_scr = rest
    ts, width = x_ref.shape
    n_grp = ts // SUBLANES
    step = pl.program_id(1)
    tile = pl.num_programs(1) - 1 - step if rev else step

    @pl.when(step == 0)
    def _():
        carry_scr[...] = jnp.zeros_like(carry_scr)

    has_prev = (tile > 0).astype(F32)
    has_next = (tile < pl.num_programs(1) - 1).astype(F32)
    x = x_ref[...]
    row = lax.broadcasted_iota(jnp.int32, (ts, width), 0)
    p6 = xp_ref[SUBLANES - 2:SUBLANES - 1, :] * has_prev
    p7 = xp_ref[SUBLANES - 1:SUBLANES, :] * has_prev
    n0 = xn_ref[0:1, :] * has_next
    xm1 = jnp.where(row == 0, p7, pltpu.roll(x, 1, 0))
    xm2 = jnp.where(row == 0, p6, jnp.where(row == 1, p7, pltpu.roll(x, 2, 0)))
    xp1 = jnp.where(row == ts - 1, n0, pltpu.roll(x, ts - 1, 0))
    xc = (cb_ref[...] + cw_ref[0:1, :] * xm2 + cw_ref[1:2, :] * xm1
          + cw_ref[2:3, :] * x + cw_ref[3:4, :] * xp1)
    lam = lam_ref[...]
    sp = jnp.maximum(-lam, 0.0) + jnp.log1p(jnp.exp(-jnp.abs(lam)))
    sub = lax.broadcasted_iota(jnp.int32, (ts, HEAD_DIM), 0) & (SUBLANES - 1)
    for n in range(width // HEAD_DIM):
        sl = slice(n * HEAD_DIM, (n + 1) * HEAD_DIM)
        xn_blk = xc[:, sl]
        zz = _dot(xn_blk.astype(BF16), w_ref[n])
        r = _sigmoid(zz[:, :HEAD_DIM] + bias_ref[0:1, sl])
        ig = _sigmoid(zz[:, HEAD_DIM:] + bias_ref[1:2, sl])
        log_a = (-RG_C) * r * sp[:, sl]
        a = jnp.exp(log_a)
        b = jnp.sqrt(-jnp.tanh(log_a) * (a * a + 1.0)) * (ig * xn_blk)
        for d in (1, 2, 4):
            if rev:
                a_s, b_s, ok = pltpu.roll(a, ts - d, 0), pltpu.roll(b, ts - d, 0), sub < SUBLANES - d
            else:
                a_s, b_s, ok = pltpu.roll(a, d, 0), pltpu.roll(b, d, 0), sub >= d
            b = jnp.where(ok, a * b_s + b, b)
            a = jnp.where(ok, a * a_s, a)
        h_in = jnp.broadcast_to(carry_scr[:, sl], (SUBLANES, HEAD_DIM))
        last = 0 if rev else SUBLANES - 1
        for g in (range(n_grp - 1, -1, -1) if rev else range(n_grp)):
            rows = slice(g * SUBLANES, (g + 1) * SUBLANES)
            h = a[rows, :] * h_in + b[rows, :]
            h_in = jnp.broadcast_to(h[last:last + 1, :], (SUBLANES, HEAD_DIM))
            if rev:
                o_ref[rows, sl] = (_gelu_tanh(gate_ref[rows, sl]) * (hf_ref[rows, sl] + h)).astype(o_ref.dtype)
            else:
                o_ref[rows, sl] = h
        carry_scr[:, sl] = h_in[0:1, :]


def _rglru_dir(u3, col0, conv_w, conv_b, w_dir, bias_dir, lam_dir, b_w, hf=None):
    bsz, s_len, _ = u3.shape
    rev = hf is not None
    ts = min(RG_ROWS, s_len)
    nt = s_len // ts
    cb = col0 // b_w
    rows8 = ts // SUBLANES

    def tile(i):
        return nt - 1 - i if rev else i

    in_specs = [pl.BlockSpec((None, ts, b_w), lambda b, i: (b, tile(i), cb)),
                pl.BlockSpec((None, SUBLANES, b_w), lambda b, i: (b, jnp.maximum(tile(i) * rows8 - 1, 0), cb)),
                pl.BlockSpec((None, SUBLANES, b_w),
                             lambda b, i: (b, jnp.minimum((tile(i) + 1) * rows8, s_len // SUBLANES - 1), cb)),
                pl.BlockSpec((4, b_w), lambda b, i: (0, 0)),
                pl.BlockSpec((1, b_w), lambda b, i: (0, 0)),
                pl.BlockSpec(w_dir.shape, lambda b, i: (0, 0, 0)),
                pl.BlockSpec((2, b_w), lambda b, i: (0, 0)),
                pl.BlockSpec((1, b_w), lambda b, i: (0, 0))]
    args = [u3, u3, u3, conv_w, conv_b.reshape(1, b_w), w_dir, bias_dir, lam_dir.reshape(1, b_w)]
    if rev:
        in_specs += [pl.BlockSpec((None, ts, b_w), lambda b, i: (b, tile(i), cb + 1)),
                     pl.BlockSpec((None, ts, b_w), lambda b, i: (b, tile(i), 0))]
        args += [u3, hf]
    return pl.pallas_call(
        functools.partial(_rglru_kernel, rev=rev),
        grid=(bsz, nt),
        in_specs=in_specs,
        out_specs=pl.BlockSpec((None, ts, b_w), lambda b, i: (b, tile(i), 0)),
        out_shape=jax.ShapeDtypeStruct((bsz, s_len, b_w), BF16 if rev else F32),
        scratch_shapes=[pltpu.VMEM((1, b_w), F32)],
        compiler_params=_params(("parallel", "arbitrary"), 2 * 4 * ts * b_w * 4 + (24 << 20)),
        name="rglru_bwd" if rev else "rglru_fwd",
    )(*args)


def _attn_prep_kernel(q0_ref, q1_ref, kv_ref, cos_ref, sa_ref, sb_ref, qn_ref, kn_ref,
                      qo_ref, ko_ref, vo_ref):
    cos = cos_ref[...]
    sin_a = sa_ref[...]
    sin_b = sb_ref[...]

    def norm_rope(t, w):
        y = t * lax.rsqrt(jnp.mean(t * t, axis=-1, keepdims=True) + RMS_EPS) * w
        return (y * cos + pltpu.roll(y, HEAD_DIM - ROPE_HALF // 2, 1) * sin_a
                + pltpu.roll(y, ROPE_HALF // 2, 1) * sin_b)

    half_heads = q0_ref.shape[1] // HEAD_DIM
    kv_heads = ko_ref.shape[1] // HEAD_DIM
    qn = qn_ref[...]
    kn = kn_ref[...]
    for hd in range(half_heads):
        sl = slice(hd * HEAD_DIM, (hd + 1) * HEAD_DIM)
        so = slice((half_heads + hd) * HEAD_DIM, (half_heads + hd + 1) * HEAD_DIM)
        qo_ref[:, sl] = norm_rope(q0_ref[:, sl], qn).astype(BF16)
        qo_ref[:, so] = norm_rope(q1_ref[:, sl], qn).astype(BF16)
    for hd in range(kv_heads):
        sl = slice(hd * HEAD_DIM, (hd + 1) * HEAD_DIM)
        ko_ref[:, sl] = norm_rope(kv_ref[:, sl], kn).astype(BF16)
    vo_ref[...] = kv_ref[:, kv_heads * HEAD_DIM:].T.astype(BF16)


def _attn_prep(u2, col0, c_w, kv_w, s_len, cos, sin_a, sin_b, qn_w, kn_w):
    t = u2.shape[0]
    tr = min(PREP_ROWS, s_len)
    half = c_w // 2
    assert col0 % half == 0 and 2 * kv_w == half
    cb = col0 // half
    npos = s_len // tr
    tab = pl.BlockSpec((tr, HEAD_DIM), lambda i: (i % npos, 0))
    vec = pl.BlockSpec((1, HEAD_DIM), lambda i: (0, 0))
    return pl.pallas_call(
        _attn_prep_kernel,
        grid=(t // tr,),
        in_specs=[pl.BlockSpec((tr, half), lambda i: (i, cb)),
                  pl.BlockSpec((tr, half), lambda i: (i, cb + 1)),
                  pl.BlockSpec((tr, half), lambda i: (i, cb + 2)),
                  tab, tab, tab, vec, vec],
        out_specs=[pl.BlockSpec((tr, c_w), lambda i: (i, 0)),
                   pl.BlockSpec((tr, kv_w), lambda i: (i, 0)),
                   pl.BlockSpec((kv_w, tr), lambda i: (0, i))],
        out_shape=[jax.ShapeDtypeStruct((t, c_w), BF16),
                   jax.ShapeDtypeStruct((t, kv_w), BF16),
                   jax.ShapeDtypeStruct((kv_w, t), BF16)],
        compiler_params=_params(("parallel",), 2 * tr * (3 * half * 4 + (c_w + 2 * kv_w) * 2) + (16 << 20)),
        name="attn_prep",
    )(u2, u2, u2, cos, sin_a, sin_b, qn_w.reshape(1, HEAD_DIM), kn_w.reshape(1, HEAD_DIM))


def _attn_kernel(q_ref, k_ref, vt_ref, o_ref, sc_scr):
    k = k_ref[...]
    vt = vt_ref[...]
    vt_aug = jnp.concatenate([vt, jnp.ones((BF16_ROWS, vt.shape[1]), BF16)], axis=0)
    s_len = k.shape[0]
    kc = min(ATTN_KB, s_len)
    n_kc = s_len // kc

    def scores(g, c, run_max):
        rows = slice(c * kc, (c + 1) * kc)
        sc = _dot_nt(k[rows, :], q_ref[:, g * HEAD_DIM:(g + 1) * HEAD_DIM])
        sc_scr[g % 2, rows, :] = sc
        cmax = jnp.max(sc.reshape(kc // SUBLANES, SUBLANES, sc.shape[1]), axis=0)
        return cmax if run_max is None else jnp.maximum(run_max, cmax)

    next_max = None
    for c in range(n_kc):
        next_max = scores(0, c, next_max)
    for g in range(C_GROUP):
        mx = jnp.max(next_max, axis=0, keepdims=True)
        next_max = None
        acc = None
        for c in range(n_kc):
            rows = slice(c * kc, (c + 1) * kc)
            p = jnp.exp2((sc_scr[g % 2, rows, :] - mx) * (ATTN_SCALE * LOG2_E)).astype(BF16)
            if g + 1 < C_GROUP:
                next_max = scores(g + 1, c, next_max)
            part = _dot(vt_aug[:, rows], p)
            acc = part if acc is None else acc + part
        ot = acc[:HEAD_DIM, :] / acc[HEAD_DIM:HEAD_DIM + 1, :]
        o_ref[:, g * HEAD_DIM:(g + 1) * HEAD_DIM] = ot.T.astype(o_ref.dtype)


def _attention(qr, kr, vt, bsz, s_len):
    t, c_w = qr.shape
    kv_heads = kr.shape[1] // HEAD_DIM
    gw = C_GROUP * HEAD_DIM
    tq = min(ATTN_TQ, s_len)
    nq = s_len // tq
    return pl.pallas_call(
        _attn_kernel,
        grid=(bsz, kv_heads, nq),
        in_specs=[pl.BlockSpec((tq, gw), lambda b, h, i: (b * nq + i, h)),
                  pl.BlockSpec((s_len, HEAD_DIM), lambda b, h, i: (b, h)),
                  pl.BlockSpec((HEAD_DIM, s_len), lambda b, h, i: (h, b))],
        out_specs=pl.BlockSpec((tq, gw), lambda b, h, i: (b * nq + i, h)),
        out_shape=jax.ShapeDtypeStruct((t, c_w), BF16),
        scratch_shapes=[pltpu.VMEM((2, s_len, tq), F32)],
        compiler_params=_params(("parallel", "parallel", "arbitrary"),
                                8 * tq * s_len * 4 + 4 * s_len * HEAD_DIM * 2 + (8 << 20)),
        name="attention",
    )(qr, kr, vt)


def _outproj_kernel(ya_ref, yb_ref, yc_ref, wa_ref, wb_ref, wc_ref, res_ref, o_ref, *, alpha):
    acc = _dot(ya_ref[...], wa_ref[...])
    acc = acc + _dot(yb_ref[...], wb_ref[...])
    acc = acc + _dot(yc_ref[...], wc_ref[...])
    o_ref[...] = alpha * res_ref[...] + acc


def _outproj(ya, yb, yc, w_out, layer, res, alpha):
    t, a_w = ya.shape
    b_w = yb.shape[1]
    c_w = yc.shape[1]
    n = w_out.shape[2]
    assert a_w == b_w and (a_w + b_w) % c_w == 0
    tm = min(OUT_TM, t)
    tn = min(MM_TN, n)
    c_blk = (a_w + b_w) // c_w
    vm = 2 * (tm * (a_w + b_w + c_w) * 2 + (a_w + b_w + c_w) * tn * 2 + 2 * tm * tn * 4) + tm * tn * 4 + (4 << 20)
    return pl.pallas_call(
        functools.partial(_outproj_kernel, alpha=alpha),
        grid=(t // tm, n // tn),
        in_specs=[pl.BlockSpec((tm, a_w), lambda i, j: (i, 0)),
                  pl.BlockSpec((tm, b_w), lambda i, j: (i, 0)),
                  pl.BlockSpec((tm, c_w), lambda i, j: (i, 0)),
                  pl.BlockSpec((None, a_w, tn), lambda i, j: (layer, 0, j)),
                  pl.BlockSpec((None, b_w, tn), lambda i, j: (layer, 1, j)),
                  pl.BlockSpec((None, c_w, tn), lambda i, j: (layer, c_blk, j)),
                  pl.BlockSpec((tm, tn), lambda i, j: (i, j))],
        out_specs=pl.BlockSpec((tm, tn), lambda i, j: (i, j)),
        out_shape=jax.ShapeDtypeStruct((t, n), F32),
        compiler_params=_params(("parallel", "arbitrary"), vm),
        name="outproj",
    )(ya, yb, yc, w_out, w_out, w_out, res)


def _ffn_up_kernel(x_ref, wg_ref, wu_ref, gp_ref, gn_ref, cw_ref, cb_ref, o_ref, *, tiles_per_seq):
    tm, tn = o_ref.shape
    x = x_ref[...]
    g = _dot(x, wg_ref[...])
    up = _dot(x, wu_ref[...])
    pos = pl.program_id(0) % tiles_per_seq
    has_prev = (pos > 0).astype(F32)
    has_next = (pos < tiles_per_seq - 1).astype(F32)
    g_prev = gp_ref[0, 1:2, :] * has_prev
    g_next = gn_ref[0, 0:1, :] * has_next
    row = lax.broadcasted_iota(jnp.int32, (tm, tn), 0)
    g_m1 = jnp.where(row == 0, g_prev, pltpu.roll(g, 1, 0))
    g_p1 = jnp.where(row == tm - 1, g_next, pltpu.roll(g, tm - 1, 0))
    y = cb_ref[...] + cw_ref[0:1, :] * g_m1 + cw_ref[1:2, :] * g + cw_ref[2:3, :] * g_p1
    o_ref[...] = ((y * _sigmoid(y)) * up).astype(o_ref.dtype)


def _ffn_up(hbf, w_up, layer, conv_w, conv_b, s_len):
    t, d = hbf.shape
    d_ff = w_up.shape[2] // 2
    tm = min(FFN_TM, s_len)
    tn = FFN_TN
    assert d_ff % tn == 0
    nj = d_ff // tn
    n_tiles = t // tm
    tiles_per_seq = s_len // tm
    edge_x = hbf.reshape(n_tiles, tm, d)[:, (0, tm - 1), :].reshape(2 * n_tiles, d)
    g_edge = _matmul(edge_x, w_up, layer, 2 * n_tiles, tn, F32, "ffn_edge", n_cols=d_ff)
    g_edge = g_edge.reshape(n_tiles, 2, d_ff)
    vm = 2 * (tm * d * 2 + 2 * d * tn * 2 + tm * tn * 2) + 8 * tm * tn * 4 + (4 << 20)
    return pl.pallas_call(
        functools.partial(_ffn_up_kernel, tiles_per_seq=tiles_per_seq),
        grid=(n_tiles, nj),
        in_specs=[pl.BlockSpec((tm, d), lambda i, j: (i, 0)),
                  pl.BlockSpec((None, d, tn), lambda i, j: (layer, 0, j)),
                  pl.BlockSpec((None, d, tn), lambda i, j: (layer, 0, nj + j)),
                  pl.BlockSpec((1, 2, tn), lambda i, j: (jnp.maximum(i - 1, 0), 0, j)),
                  pl.BlockSpec((1, 2, tn), lambda i, j: (jnp.minimum(i + 1, n_tiles - 1), 0, j)),
                  pl.BlockSpec((3, tn), lambda i, j: (0, j)),
                  pl.BlockSpec((1, tn), lambda i, j: (0, j))],
        out_specs=pl.BlockSpec((tm, tn), lambda i, j: (i, j)),
        out_shape=jax.ShapeDtypeStruct((t, d_ff), BF16),
        compiler_params=_params(("parallel", "arbitrary"), vm),
        name="ffn_up",
    )(hbf, w_up, w_up, g_edge, g_edge, conv_w, conv_b.reshape(1, d_ff))


def _rope_tables(s_len):
    rows = s_len // GRID_W
    g_r, g_c = jnp.meshgrid(jnp.arange(rows), jnp.arange(GRID_W), indexing='ij')
    row = g_r.reshape(s_len).astype(F32)
    colp = g_c.reshape(s_len).astype(F32)
    inv_freq = ROPE_THETA ** (-jnp.arange(0, ROPE_HALF, 2, dtype=F32) / ROPE_HALF)
    ang_r = row[:, None] * inv_freq[None, :]
    ang_c = colp[:, None] * inv_freq[None, :]
    ang = jnp.concatenate([ang_r, ang_r, ang_c, ang_c], axis=-1)
    cos, sin = jnp.cos(ang), jnp.sin(ang)
    first_quarter = (jnp.arange(HEAD_DIM) & (ROPE_HALF // 2)) == 0
    sin_a = jnp.where(first_quarter, -sin, 0.0)
    sin_b = jnp.where(first_quarter, 0.0, sin)
    return cos, sin_a, sin_b


def kernel(x, emb_ln_w, emb_ln_b, w_in, hgrn_lb_logits, hgrn_norm_w, rglru_conv_w, rglru_conv_b,
           rglru_wa, rglru_ba, rglru_wx, rglru_bx, rglru_lambda, attn_q_norm_w, attn_k_norm_w,
           w_out, ln1_w, ln1_b, ffn_w_up, ffn_conv_w, ffn_conv_b, ffn_w_down, ln2_w, ln2_b):
    bsz, s_len, d_model = x.shape
    depth = w_in.shape[0]
    t = bsz * s_len
    a_w = hgrn_norm_w.shape[1]
    b_w = rglru_conv_w.shape[2]
    in_cols = w_in.shape[2]
    a_cols = 5 * a_w
    b_cols = 2 * b_w
    c_w = w_out.shape[1] - a_w - b_w
    kv_w = (in_cols - a_cols - b_cols - c_w) // 2
    alpha = (2.0 * depth) ** 0.25

    cos, sin_a, sin_b = _rope_tables(s_len)
    lb_cs = jnp.cumsum(jax.nn.softmax(hgrn_lb_logits.astype(F32), axis=0), axis=0)
    lower_bounds = lb_cs - lb_cs[0:1]

    w_in_bf = w_in.astype(BF16)
    w_out_bf = w_out.astype(BF16)
    w_up_bf = ffn_w_up.astype(BF16)
    w_down_bf = ffn_w_down.astype(BF16)

    h, hbf = _layernorm(x.reshape(t, d_model), emb_ln_w, emb_ln_b)
    for l in range(depth):
        u2 = _matmul(hbf, w_in_bf, l, MM_TM, MM_TN, F32, "in_proj")
        u3 = u2.reshape(bsz, s_len, in_cols)

        ya = _hgrn2(u3, lower_bounds[l], hgrn_norm_w[l], a_w)

        yb = None
        for d in range(2):
            w_dir = jnp.concatenate([rglru_wa[l, d], rglru_wx[l, d]], axis=-1).astype(BF16)
            bias_dir = jnp.stack([rglru_ba[l, d], rglru_bx[l, d]])
            yb = _rglru_dir(u3, a_cols, rglru_conv_w[l], rglru_conv_b[l], w_dir, bias_dir,
                            rglru_lambda[l, d], b_w, hf=yb)
        yb = yb.reshape(t, b_w)

        qr, kr, vt = _attn_prep(u2, a_cols + b_cols, c_w, kv_w, s_len, cos, sin_a, sin_b,
                                attn_q_norm_w[l], attn_k_norm_w[l])
        yc = _attention(qr, kr, vt, bsz, s_len)

        pre = _outproj(ya, yb, yc, w_out_bf, l, h, alpha)
        h, hbf = _layernorm(pre, ln1_w[l], ln1_b[l])

        act = _ffn_up(hbf, w_up_bf, l, ffn_conv_w[l], ffn_conv_b[l], s_len)
        pre = _matmul_residual(act, w_down_bf, l, h, alpha, DOWN_TM, DOWN_TN, "ffn_down")
        h, hbf = _layernorm(pre, ln2_w[l], ln2_b[l])
    return h.reshape(bsz, s_len, d_model)
```

```python
import functools
import math

import jax
import jax.numpy as jnp
from jax import lax
from jax.experimental import pallas as pl
from jax.experimental.pallas import tpu as pltpu

F32 = jnp.float32
BF16 = jnp.bfloat16

HEAD_DIM = 128
RG_C = 8.0
C_GROUP = 4
ROPE_THETA = 10000.0
ROPE_HALF = HEAD_DIM // 2
GRID_W = 64
LN_EPS = 1e-5
RMS_EPS = 1e-6
ATTN_SCALE = HEAD_DIM ** -0.5
LOG2_E = math.log2(math.e)

V7X_VMEM_BYTES = 64 * 1024 * 1024
LANES = 128
SUBLANES = 8
BF16_ROWS = 16

HG_CHUNK = 128
HG_HEADS_PER_STEP = 4
LN_ROWS = 256
MM_TM = 1024
MM_TN = 1024
OUT_TM = 1024
FFN_TM = 1024
FFN_TN = 256
DOWN_TM = 512
DOWN_TN = 512
ATTN_TQ = 512
ATTN_KB = 256
PREP_ROWS = 512
RG_ROWS = 512


def _vmem_limit(nbytes):
    return int(min(V7X_VMEM_BYTES - 4 * 1024 * 1024, max(nbytes, 16 * 1024 * 1024)))


def _params(sem, vmem_bytes):
    return pltpu.CompilerParams(dimension_semantics=sem, vmem_limit_bytes=_vmem_limit(vmem_bytes))


def _dot(a, b):
    return jnp.dot(a, b, preferred_element_type=F32)


def _dot_nt(a, b):
    return lax.dot_general(a, b, (((1,), (1,)), ((), ())), preferred_element_type=F32)


def _dot_tn(a, b):
    return lax.dot_general(a, b, (((0,), (0,)), ((), ())), preferred_element_type=F32)


def _sigmoid(x):
    return 1.0 / (1.0 + jnp.exp(-x))


def _ln_kernel(x_ref, w_ref, b_ref, o_ref, obf_ref):
    x = x_ref[...]
    mu = jnp.mean(x, axis=-1, keepdims=True)
    xc = x - mu
    var = jnp.mean(xc * xc, axis=-1, keepdims=True)
    y = xc * lax.rsqrt(var + LN_EPS) * w_ref[...] + b_ref[...]
    o_ref[...] = y
    obf_ref[...] = y.astype(BF16)


def _layernorm(x, w, b):
    t, d = x.shape
    tr = min(LN_ROWS, t)
    return pl.pallas_call(
        _ln_kernel,
        grid=(t // tr,),
        in_specs=[pl.BlockSpec((tr, d), lambda i: (i, 0)),
                  pl.BlockSpec((1, d), lambda i: (0, 0)),
                  pl.BlockSpec((1, d), lambda i: (0, 0))],
        out_specs=[pl.BlockSpec((tr, d), lambda i: (i, 0)),
                   pl.BlockSpec((tr, d), lambda i: (i, 0))],
        out_shape=[jax.ShapeDtypeStruct((t, d), F32), jax.ShapeDtypeStruct((t, d), BF16)],
        compiler_params=_params(("parallel",), 2 * tr * d * 10 + (8 << 20)),
        name="layernorm",
    )(x, w.reshape(1, d), b.reshape(1, d))


def _mm_kernel(x_ref, w_ref, o_ref):
    o_ref[...] = _dot(x_ref[...], w_ref[...].astype(BF16)).astype(o_ref.dtype)


def _matmul(x, w, layer, tm, tn, out_dtype, name, n_cols=None):
    m, k = x.shape
    n = w.shape[2] if n_cols is None else n_cols
    tm, tn = min(tm, m), min(tn, n)
    vm = 2 * (tm * k * 2 + k * tn * w.dtype.itemsize + tm * tn * 4) + tm * tn * 4 + k * tn * 2 + (4 << 20)
    return pl.pallas_call(
        _mm_kernel,
        grid=(m // tm, n // tn),
        in_specs=[pl.BlockSpec((tm, k), lambda i, j: (i, 0)),
                  pl.BlockSpec((None, k, tn), lambda i, j: (layer, 0, j))],
        out_specs=pl.BlockSpec((tm, tn), lambda i, j: (i, j)),
        out_shape=jax.ShapeDtypeStruct((m, n), out_dtype),
        compiler_params=_params(("parallel", "arbitrary"), vm),
        name=name,
    )(x, w)


def _mm_res_kernel(x_ref, w_ref, r_ref, o_ref, *, alpha):
    o_ref[...] = alpha * r_ref[...] + _dot(x_ref[...], w_ref[...])


def _matmul_residual(x, w, layer, res, alpha, tm, tn, name):
    m, k = x.shape
    n = w.shape[2]
    tm, tn = min(tm, m), min(tn, n)
    vm = 2 * (tm * k * 2 + k * tn * 2 + 2 * tm * tn * 4) + tm * tn * 4 + (4 << 20)
    return pl.pallas_call(
        functools.partial(_mm_res_kernel, alpha=alpha),
        grid=(m // tm, n // tn),
        in_specs=[pl.BlockSpec((tm, k), lambda i, j: (i, 0)),
                  pl.BlockSpec((None, k, tn), lambda i, j: (layer, 0, j)),
                  pl.BlockSpec((tm, tn), lambda i, j: (i, j))],
        out_specs=pl.BlockSpec((tm, tn), lambda i, j: (i, j)),
        out_shape=jax.ShapeDtypeStruct((m, n), F32),
        compiler_params=_params(("parallel", "arbitrary"), vm),
        name=name,
    )(x, w, res)


def _hgrn_chunk(q, z, v, lb, st_ref, rev, row, pair_code):
    c_len = q.shape[0]
    n_lvl = c_len.bit_length() - 1
    t = jnp.exp(-jnp.abs(z))
    r = 1.0 / (1.0 + t)
    tr = t * r
    pos = z >= 0.0
    f = lb + (1.0 - lb) * jnp.where(pos, r, tr)
    kk = (1.0 - lb) * jnp.where(pos, tr, r)

    vb = v.astype(BF16)
    scores = _dot_nt(q.astype(BF16), kk.astype(BF16))
    ep = f
    ex = jnp.ones_like(f)
    et = f
    for lvl in range(n_lvl):
        m = 1 << lvl
        later = ((row & m) == 0) if rev else ((row & m) != 0)
        x = jnp.where(later, q * ep, kk * ex).astype(BF16)
        scores = jnp.where(pair_code >= m, _dot_nt(x, x), scores)
        from_earlier, from_later = (c_len - m, m) if rev else (m, c_len - m)
        sib = jnp.where(later, pltpu.roll(et, from_earlier, 0), pltpu.roll(et, from_later, 0))
        ep = jnp.where(later, ep * sib, ep)
        ex = jnp.where(later, ex, ex * sib)
        et = et * sib
    scores = jnp.where(pair_code < 0, 0.0, scores)

    st = st_ref[...]
    o = _dot_nt((q * ep).astype(BF16), st.astype(BF16)) + _dot(scores.astype(BF16), vb)
    st_ref[...] = et[0:1, :] * st + _dot_tn(vb, (kk * ex).astype(BF16))
    return o


def _hgrn_kernel(q_ref, zf_ref, zb_ref, v_ref, g_ref, lb_ref, nw_ref, o_ref,
                 of_scr, ob_scr, stf_scr, stb_scr, *, chunk):
    s_len = q_ref.shape[0]
    n_chunks = s_len // chunk
    row = lax.broadcasted_iota(jnp.int32, (chunk, HEAD_DIM), 0)
    ri = lax.broadcasted_iota(jnp.int32, (chunk, chunk), 0)
    ci = lax.broadcasted_iota(jnp.int32, (chunk, chunk), 1)
    code_f = jnp.where(ri >= ci, ri ^ ci, -1)
    code_b = jnp.where(ci >= ri, ri ^ ci, -1)
    n_heads = q_ref.shape[1] // HEAD_DIM
    stf_scr[...] = jnp.zeros_like(stf_scr)
    stb_scr[...] = jnp.zeros_like(stb_scr)

    def body(i, carry):
        rf = pl.multiple_of(i * chunk, chunk)
        rb = pl.multiple_of((n_chunks - 1 - i) * chunk, chunk)
        for hd in range(n_heads):
            sl = slice(hd * HEAD_DIM, (hd + 1) * HEAD_DIM)
            of_scr[pl.ds(rf, chunk), sl] = _hgrn_chunk(
                q_ref[pl.ds(rf, chunk), sl], zf_ref[pl.ds(rf, chunk), sl], v_ref[pl.ds(rf, chunk), sl],
                lb_ref[0:1, sl], stf_scr.at[hd], False, row, code_f)
            ob_scr[pl.ds(rb, chunk), sl] = _hgrn_chunk(
                q_ref[pl.ds(rb, chunk), sl], zb_ref[pl.ds(rb, chunk), sl], v_ref[pl.ds(rb, chunk), sl],
                lb_ref[1:2, sl], stb_scr.at[hd], True, row, code_b)
        return carry

    lax.fori_loop(0, n_chunks, body, 0)

    def epilogue(i, carry):
        r = pl.multiple_of(i * chunk, chunk)
        for hd in range(n_heads):
            sl = slice(hd * HEAD_DIM, (hd + 1) * HEAD_DIM)
            o = of_scr[pl.ds(r, chunk), sl] + ob_scr[pl.ds(r, chunk), sl]
            y = o * lax.rsqrt(jnp.mean(o * o, axis=-1, keepdims=True) + RMS_EPS) * nw_ref[:, sl]
            g = g_ref[pl.ds(r, chunk), sl]
            o_ref[pl.ds(r, chunk), sl] = (y * (g * _sigmoid(g))).astype(o_ref.dtype)
        return carry

    lax.fori_loop(0, n_chunks, epilogue, 0)


def _hgrn2(u3, lb, norm_w, a_w):
    bsz, s_len, _ = u3.shape
    heads = a_w // HEAD_DIM
    chunk = min(HG_CHUNK, s_len)
    hps = HG_HEADS_PER_STEP
    wid = hps * HEAD_DIM
    groups = heads // hps

    def col(k):
        return pl.BlockSpec((None, s_len, wid), lambda b, h: (b, 0, k * groups + h))

    out = pl.pallas_call(
        functools.partial(_hgrn_kernel, chunk=chunk),
        grid=(bsz, groups),
        in_specs=[col(0), col(1), col(2), col(3), col(4),
                  pl.BlockSpec((2, wid), lambda b, h: (0, h)),
                  pl.BlockSpec((1, wid), lambda b, h: (0, h))],
        out_specs=pl.BlockSpec((None, s_len, wid), lambda b, h: (b, 0, h)),
        out_shape=jax.ShapeDtypeStruct((bsz, s_len, a_w), BF16),
        scratch_shapes=[pltpu.VMEM((s_len, wid), F32), pltpu.VMEM((s_len, wid), F32),
                        pltpu.VMEM((hps, HEAD_DIM, HEAD_DIM), F32), pltpu.VMEM((hps, HEAD_DIM, HEAD_DIM), F32)],
        compiler_params=_params(("parallel", "arbitrary"), 14 * s_len * wid * 4 + (16 << 20)),
        name="hgrn2",
    )(u3, u3, u3, u3, u3, lb, norm_w.reshape(1, a_w))
    return out.reshape(bsz * s_len, a_w)


def _gelu_tanh(x):
    return 0.5 * x * (1.0 + jnp.tanh(math.sqrt(2.0 / math.pi) * (x + 0.044715 * (x * x * x))))


def _rglru_kernel(x_ref, xp_ref, xn_ref, cw_ref, cb_ref, w_ref, bias_ref, lam_ref, *rest, rev):
    if rev:
        gate_ref, hf_ref, o_ref, carry_scr = rest
    else:
        o_ref, carry_scr = rest
    ts, width = x_ref.shape
    n_grp = ts // SUBLANES
    step = pl.program_id(1)
    tile = pl.num_programs(1) - 1 - step if rev else step

    @pl.when(step == 0)
    def _():
        carry_scr[...] = jnp.zeros_like(carry_scr)

    has_prev = (tile > 0).astype(F32)
    has_next = (tile < pl.num_programs(1) - 1).astype(F32)
    x = x_ref[...]
    row = lax.broadcasted_iota(jnp.int32, (ts, width), 0)
    p6 = xp_ref[SUBLANES - 2:SUBLANES - 1, :] * has_prev
    p7 = xp_ref[SUBLANES - 1:SUBLANES, :] * has_prev
    n0 = xn_ref[0:1, :] * has_next
    xm1 = jnp.where(row == 0, p7, pltpu.roll(x, 1, 0))
    xm2 = jnp.where(row == 0, p6, jnp.where(row == 1, p7, pltpu.roll(x, 2, 0)))
    xp1 = jnp.where(row == ts - 1, n0, pltpu.roll(x, ts - 1, 0))
    xc = (cb_ref[...] + cw_ref[0:1, :] * xm2 + cw_ref[1:2, :] * xm1
          + cw_ref[2:3, :] * x + cw_ref[3:4, :] * xp1)
    lam = lam_ref[...]
    sp = jnp.maximum(-lam, 0.0) + jnp.log1p(jnp.exp(-jnp.abs(lam)))
    sub = lax.broadcasted_iota(jnp.int32, (ts, HEAD_DIM), 0) & (SUBLANES - 1)
    for n in range(width // HEAD_DIM):
        sl = slice(n * HEAD_DIM, (n + 1) * HEAD_DIM)
        xn_blk = xc[:, sl]
        zz = _dot(xn_blk.astype(BF16), w_ref[n])
        r = _sigmoid(zz[:, :HEAD_DIM] + bias_ref[0:1, sl])
        ig = _sigmoid(zz[:, HEAD_DIM:] + bias_ref[1:2, sl])
        log_a = (-RG_C) * r * sp[:, sl]
        a = jnp.exp(log_a)
        b = jnp.sqrt(-jnp.tanh(log_a) * (a * a + 1.0)) * (ig * xn_blk)
        for d in (1, 2, 4):
            if rev:
                a_s, b_s, ok = pltpu.roll(a, ts - d, 0), pltpu.roll(b, ts - d, 0), sub < SUBLANES - d
            else:
                a_s, b_s, ok = pltpu.roll(a, d, 0), pltpu.roll(b, d, 0), sub >= d
            b = jnp.where(ok, a * b_s + b, b)
            a = jnp.where(ok, a * a_s, a)
        h_in = jnp.broadcast_to(carry_scr[:, sl], (SUBLANES, HEAD_DIM))
        last = 0 if rev else SUBLANES - 1
        for g in (range(n_grp - 1, -1, -1) if rev else range(n_grp)):
            rows = slice(g * SUBLANES, (g + 1) * SUBLANES)
            h = a[rows, :] * h_in + b[rows, :]
            h_in = jnp.broadcast_to(h[last:last + 1, :], (SUBLANES, HEAD_DIM))
            if rev:
                o_ref[rows, sl] = (_gelu_tanh(gate_ref[rows, sl]) * (hf_ref[rows, sl] + h)).astype(o_ref.dtype)
            else:
                o_ref[rows, sl] = h
        carry_scr[:, sl] = h_in[0:1, :]


def _rglru_dir(u3, col0, conv_w, conv_b, w_dir, bias_dir, lam_dir, b_w, hf=None):
    bsz, s_len, _ = u3.shape
    rev = hf is not None
    ts = min(RG_ROWS, s_len)
    nt = s_len // ts
    cb = col0 // b_w
    rows8 = ts // SUBLANES

    def tile(i):
        return nt - 1 - i if rev else i

    in_specs = [pl.BlockSpec((None, ts, b_w), lambda b, i: (b, tile(i), cb)),
                pl.BlockSpec((None, SUBLANES, b_w), lambda b, i: (b, jnp.maximum(tile(i) * rows8 - 1, 0), cb)),
                pl.BlockSpec((None, SUBLANES, b_w),
                             lambda b, i: (b, jnp.minimum((tile(i) + 1) * rows8, s_len // SUBLANES - 1), cb)),
                pl.BlockSpec((4, b_w), lambda b, i: (0, 0)),
                pl.BlockSpec((1, b_w), lambda b, i: (0, 0)),
                pl.BlockSpec(w_dir.shape, lambda b, i: (0, 0, 0)),
                pl.BlockSpec((2, b_w), lambda b, i: (0, 0)),
                pl.BlockSpec((1, b_w), lambda b, i: (0, 0))]
    args = [u3, u3, u3, conv_w, conv_b.reshape(1, b_w), w_dir, bias_dir, lam_dir.reshape(1, b_w)]
    if rev:
        in_specs += [pl.BlockSpec((None, ts, b_w), lambda b, i: (b, tile(i), cb + 1)),
                     pl.BlockSpec((None, ts, b_w), lambda b, i: (b, tile(i), 0))]
        args += [u3, hf]
    return pl.pallas_call(
        functools.partial(_rglru_kernel, rev=rev),
        grid=(bsz, nt),
        in_specs=in_specs,
        out_specs=pl.BlockSpec((None, ts, b_w), lambda b, i: (b, tile(i), 0)),
        out_shape=jax.ShapeDtypeStruct((bsz, s_len, b_w), BF16 if rev else F32),
        scratch_shapes=[pltpu.VMEM((1, b_w), F32)],
        compiler_params=_params(("parallel", "arbitrary"), 2 * 4 * ts * b_w * 4 + (24 << 20)),
        name="rglru_bwd" if rev else "rglru_fwd",
    )(*args)


def _attn_prep_kernel(q0_ref, q1_ref, kv_ref, cos_ref, sa_ref, sb_ref, qn_ref, kn_ref,
                      qo_ref, ko_ref, vo_ref):
    cos = cos_ref[...]
    sin_a = sa_ref[...]
    sin_b = sb_ref[...]

    def norm_rope(t, w):
        y = t * lax.rsqrt(jnp.mean(t * t, axis=-1, keepdims=True) + RMS_EPS) * w
        return (y * cos + pltpu.roll(y, HEAD_DIM - ROPE_HALF // 2, 1) * sin_a
                + pltpu.roll(y, ROPE_HALF // 2, 1) * sin_b)

    half_heads = q0_ref.shape[1] // HEAD_DIM
    kv_heads = ko_ref.shape[1] // HEAD_DIM
    qn = qn_ref[...]
    kn = kn_ref[...]
    for hd in range(half_heads):
        sl = slice(hd * HEAD_DIM, (hd + 1) * HEAD_DIM)
        so = slice((half_heads + hd) * HEAD_DIM, (half_heads + hd + 1) * HEAD_DIM)
        qo_ref[:, sl] = norm_rope(q0_ref[:, sl], qn).astype(BF16)
        qo_ref[:, so] = norm_rope(q1_ref[:, sl], qn).astype(BF16)
    for hd in range(kv_heads):
        sl = slice(hd * HEAD_DIM, (hd + 1) * HEAD_DIM)
        ko_ref[:, sl] = norm_rope(kv_ref[:, sl], kn).astype(BF16)
    vo_ref[...] = kv_ref[:, kv_heads * HEAD_DIM:].T.astype(BF16)


def _attn_prep(u2, col0, c_w, kv_w, s_len, cos, sin_a, sin_b, qn_w, kn_w):
    t = u2.shape[0]
    tr = min(PREP_ROWS, s_len)
    half = c_w // 2
    assert col0 % half == 0 and 2 * kv_w == half
    cb = col0 // half
    npos = s_len // tr
    tab = pl.BlockSpec((tr, HEAD_DIM), lambda i: (i % npos, 0))
    vec = pl.BlockSpec((1, HEAD_DIM), lambda i: (0, 0))
    return pl.pallas_call(
        _attn_prep_kernel,
        grid=(t // tr,),
        in_specs=[pl.BlockSpec((tr, half), lambda i: (i, cb)),
                  pl.BlockSpec((tr, half), lambda i: (i, cb + 1)),
                  pl.BlockSpec((tr, half), lambda i: (i, cb + 2)),
                  tab, tab, tab, vec, vec],
        out_specs=[pl.BlockSpec((tr, c_w), lambda i: (i, 0)),
                   pl.BlockSpec((tr, kv_w), lambda i: (i, 0)),
                   pl.BlockSpec((kv_w, tr), lambda i: (0, i))],
        out_shape=[jax.ShapeDtypeStruct((t, c_w), BF16),
                   jax.ShapeDtypeStruct((t, kv_w), BF16),
                   jax.ShapeDtypeStruct((kv_w, t), BF16)],
        compiler_params=_params(("parallel",), 2 * tr * (3 * half * 4 + (c_w + 2 * kv_w) * 2) + (16 << 20)),
        name="attn_prep",
    )(u2, u2, u2, cos, sin_a, sin_b, qn_w.reshape(1, HEAD_DIM), kn_w.reshape(1, HEAD_DIM))


def _attn_kernel(q_ref, k_ref, vt_ref, o_ref, sc_scr):
    k = k_ref[...]
    vt = vt_ref[...]
    vt_aug = jnp.concatenate([vt, jnp.ones((BF16_ROWS, vt.shape[1]), BF16)], axis=0)
    s_len = k.shape[0]
    kc = min(ATTN_KB, s_len)
    n_kc = s_len // kc

    def scores(g, c, run_max):
        rows = slice(c * kc, (c + 1) * kc)
        sc = _dot_nt(k[rows, :], q_ref[:, g * HEAD_DIM:(g + 1) * HEAD_DIM])
        sc_scr[g % 2, rows, :] = sc
        cmax = jnp.max(sc.reshape(kc // SUBLANES, SUBLANES, sc.shape[1]), axis=0)
        return cmax if run_max is None else jnp.maximum(run_max, cmax)

    next_max = None
    for c in range(n_kc):
        next_max = scores(0, c, next_max)
    for g in range(C_GROUP):
        mx = jnp.max(next_max, axis=0, keepdims=True)
        next_max = None
        acc = None
        for c in range(n_kc):
            rows = slice(c * kc, (c + 1) * kc)
            p = jnp.exp2((sc_scr[g % 2, rows, :] - mx) * (ATTN_SCALE * LOG2_E)).astype(BF16)
            if g + 1 < C_GROUP:
                next_max = scores(g + 1, c, next_max)
            part = _dot(vt_aug[:, rows], p)
            acc = part if acc is None else acc + part
        ot = acc[:HEAD_DIM, :] / acc[HEAD_DIM:HEAD_DIM + 1, :]
        o_ref[:, g * HEAD_DIM:(g + 1) * HEAD_DIM] = ot.T.astype(o_ref.dtype)


def _attention(qr, kr, vt, bsz, s_len):
    t, c_w = qr.shape
    kv_heads = kr.shape[1] // HEAD_DIM
    gw = C_GROUP * HEAD_DIM
    tq = min(ATTN_TQ, s_len)
    nq = s_len // tq
    return pl.pallas_call(
        _attn_kernel,
        grid=(bsz, kv_heads, nq),
        in_specs=[pl.BlockSpec((tq, gw), lambda b, h, i: (b * nq + i, h)),
                  pl.BlockSpec((s_len, HEAD_DIM), lambda b, h, i: (b, h)),
                  pl.BlockSpec((HEAD_DIM, s_len), lambda b, h, i: (h, b))],
        out_specs=pl.BlockSpec((tq, gw), lambda b, h, i: (b * nq + i, h)),
        out_shape=jax.ShapeDtypeStruct((t, c_w), BF16),
        scratch_shapes=[pltpu.VMEM((2, s_len, tq), F32)],
        compiler_params=_params(("parallel", "parallel", "arbitrary"),
                                8 * tq * s_len * 4 + 4 * s_len * HEAD_DIM * 2 + (8 << 20)),
        name="attention",
    )(qr, kr, vt)


def _outproj_kernel(ya_ref, yb_ref, yc_ref, wa_ref, wb_ref, wc_ref, res_ref, o_ref, *, alpha):
    acc = _dot(ya_ref[...], wa_ref[...])
    acc = acc + _dot(yb_ref[...], wb_ref[...])
    acc = acc + _dot(yc_ref[...], wc_ref[...])
    o_ref[...] = alpha * res_ref[...] + acc


def _outproj(ya, yb, yc, w_out, layer, res, alpha):
    t, a_w = ya.shape
    b_w = yb.shape[1]
    c_w = yc.shape[1]
    n = w_out.shape[2]
    assert a_w == b_w and (a_w + b_w) % c_w == 0
    tm = min(OUT_TM, t)
    tn = min(MM_TN, n)
    c_blk = (a_w + b_w) // c_w
    vm = 2 * (tm * (a_w + b_w + c_w) * 2 + (a_w + b_w + c_w) * tn * 2 + 2 * tm * tn * 4) + tm * tn * 4 + (4 << 20)
    return pl.pallas_call(
        functools.partial(_outproj_kernel, alpha=alpha),
        grid=(t // tm, n // tn),
        in_specs=[pl.BlockSpec((tm, a_w), lambda i, j: (i, 0)),
                  pl.BlockSpec((tm, b_w), lambda i, j: (i, 0)),
                  pl.BlockSpec((tm, c_w), lambda i, j: (i, 0)),
                  pl.BlockSpec((None, a_w, tn), lambda i, j: (layer, 0, j)),
                  pl.BlockSpec((None, b_w, tn), lambda i, j: (layer, 1, j)),
                  pl.BlockSpec((None, c_w, tn), lambda i, j: (layer, c_blk, j)),
                  pl.BlockSpec((tm, tn), lambda i, j: (i, j))],
        out_specs=pl.BlockSpec((tm, tn), lambda i, j: (i, j)),
        out_shape=jax.ShapeDtypeStruct((t, n), F32),
        compiler_params=_params(("parallel", "arbitrary"), vm),
        name="outproj",
    )(ya, yb, yc, w_out, w_out, w_out, res)


def _ffn_up_kernel(x_ref, wg_ref, wu_ref, gp_ref, gn_ref, cw_ref, cb_ref, o_ref, wg_scr, wu_scr,
                   *, tiles_per_seq):
    tm, tn = o_ref.shape

    @pl.when(pl.program_id(1) == 0)
    def _():
        wg_scr[...] = wg_ref[...].astype(BF16)
        wu_scr[...] = wu_ref[...].astype(BF16)

    x = x_ref[...]
    g = _dot(x, wg_scr[...])
    up = _dot(x, wu_scr[...])
    pos = pl.program_id(1) % tiles_per_seq
    has_prev = (pos > 0).astype(F32)
    has_next = (pos < tiles_per_seq - 1).astype(F32)
    g_prev = gp_ref[0, 1:2, :] * has_prev
    g_next = gn_ref[0, 0:1, :] * has_next
    row = lax.broadcasted_iota(jnp.int32, (tm, tn), 0)
    g_m1 = jnp.where(row == 0, g_prev, pltpu.roll(g, 1, 0))
    g_p1 = jnp.where(row == tm - 1, g_next, pltpu.roll(g, tm - 1, 0))
    y = cb_ref[...] + cw_ref[0:1, :] * g_m1 + cw_ref[1:2, :] * g + cw_ref[2:3, :] * g_p1
    o_ref[...] = ((y * _sigmoid(y)) * up).astype(o_ref.dtype)


def _ffn_up(hbf, w_up, layer, conv_w, conv_b, s_len):
    t, d = hbf.shape
    d_ff = w_up.shape[2] // 2
    tm = min(FFN_TM, s_len)
    tn = FFN_TN
    assert d_ff % tn == 0
    nj = d_ff // tn
    n_tiles = t // tm
    tiles_per_seq = s_len // tm
    edge_x = hbf.reshape(n_tiles, tm, d)[:, (0, tm - 1), :].reshape(2 * n_tiles, d)
    g_edge = _matmul(edge_x, w_up, layer, 2 * n_tiles, tn, F32, "ffn_edge", n_cols=d_ff)
    g_edge = g_edge.reshape(n_tiles, 2, d_ff)
    vm = 2 * (tm * d * 2 + 2 * d * tn * 4 + tm * tn * 2) + 2 * d * tn * 2 + 8 * tm * tn * 4 + (4 << 20)
    w_scr = pltpu.VMEM((d, tn), BF16)
    return pl.pallas_call(
        functools.partial(_ffn_up_kernel, tiles_per_seq=tiles_per_seq),
        grid=(nj, n_tiles),
        in_specs=[pl.BlockSpec((tm, d), lambda j, i: (i, 0)),
                  pl.BlockSpec((None, d, tn), lambda j, i: (layer, 0, j)),
                  pl.BlockSpec((None, d, tn), lambda j, i: (layer, 0, nj + j)),
                  pl.BlockSpec((1, 2, tn), lambda j, i: (jnp.maximum(i - 1, 0), 0, j)),
                  pl.BlockSpec((1, 2, tn), lambda j, i: (jnp.minimum(i + 1, n_tiles - 1), 0, j)),
                  pl.BlockSpec((3, tn), lambda j, i: (0, j)),
                  pl.BlockSpec((1, tn), lambda j, i: (0, j))],
        out_specs=pl.BlockSpec((tm, tn), lambda j, i: (i, j)),
        out_shape=jax.ShapeDtypeStruct((t, d_ff), BF16),
        scratch_shapes=[w_scr, w_scr],
        compiler_params=_params(("parallel", "arbitrary"), vm),
        name="ffn_up",
    )(hbf, w_up, w_up, g_edge, g_edge, conv_w, conv_b.reshape(1, d_ff))


def _rope_tables(s_len):
    rows = s_len // GRID_W
    g_r, g_c = jnp.meshgrid(jnp.arange(rows), jnp.arange(GRID_W), indexing='ij')
    row = g_r.reshape(s_len).astype(F32)
    colp = g_c.reshape(s_len).astype(F32)
    inv_freq = ROPE_THETA ** (-jnp.arange(0, ROPE_HALF, 2, dtype=F32) / ROPE_HALF)
    ang_r = row[:, None] * inv_freq[None, :]
    ang_c = colp[:, None] * inv_freq[None, :]
    ang = jnp.concatenate([ang_r, ang_r, ang_c, ang_c], axis=-1)
    cos, sin = jnp.cos(ang), jnp.sin(ang)
    first_quarter = (jnp.arange(HEAD_DIM) & (ROPE_HALF // 2)) == 0
    sin_a = jnp.where(first_quarter, -sin, 0.0)
    sin_b = jnp.where(first_quarter, 0.0, sin)
    return cos, sin_a, sin_b


def kernel(x, emb_ln_w, emb_ln_b, w_in, hgrn_lb_logits, hgrn_norm_w, rglru_conv_w, rglru_conv_b,
           rglru_wa, rglru_ba, rglru_wx, rglru_bx, rglru_lambda, attn_q_norm_w, attn_k_norm_w,
           w_out, ln1_w, ln1_b, ffn_w_up, ffn_conv_w, ffn_conv_b, ffn_w_down, ln2_w, ln2_b):
    bsz, s_len, d_model = x.shape
    depth = w_in.shape[0]
    t = bsz * s_len
    a_w = hgrn_norm_w.shape[1]
    b_w = rglru_conv_w.shape[2]
    in_cols = w_in.shape[2]
    a_cols = 5 * a_w
    b_cols = 2 * b_w
    c_w = w_out.shape[1] - a_w - b_w
    kv_w = (in_cols - a_cols - b_cols - c_w) // 2
    alpha = (2.0 * depth) ** 0.25

    cos, sin_a, sin_b = _rope_tables(s_len)
    lb_cs = jnp.cumsum(jax.nn.softmax(hgrn_lb_logits.astype(F32), axis=0), axis=0)
    lower_bounds = lb_cs - lb_cs[0:1]

    w_in_bf = w_in.astype(BF16)
    w_out_bf = w_out.astype(BF16)
    w_down_bf = ffn_w_down.astype(BF16)

    h, hbf = _layernorm(x.reshape(t, d_model), emb_ln_w, emb_ln_b)
    for l in range(depth):
        u2 = _matmul(hbf, w_in_bf, l, MM_TM, MM_TN, F32, "in_proj")
        u3 = u2.reshape(bsz, s_len, in_cols)

        ya = _hgrn2(u3, lower_bounds[l], hgrn_norm_w[l], a_w)

        yb = None
        for d in range(2):
            w_dir = jnp.concatenate([rglru_wa[l, d], rglru_wx[l, d]], axis=-1).astype(BF16)
            bias_dir = jnp.stack([rglru_ba[l, d], rglru_bx[l, d]])
            yb = _rglru_dir(u3, a_cols, rglru_conv_w[l], rglru_conv_b[l], w_dir, bias_dir,
                            rglru_lambda[l, d], b_w, hf=yb)
        yb = yb.reshape(t, b_w)

        qr, kr, vt = _attn_prep(u2, a_cols + b_cols, c_w, kv_w, s_len, cos, sin_a, sin_b,
                                attn_q_norm_w[l], attn_k_norm_w[l])
        yc = _attention(qr, kr, vt, bsz, s_len)

        pre = _outproj(ya, yb, yc, w_out_bf, l, h, alpha)
        h, hbf = _layernorm(pre, ln1_w[l], ln1_b[l])

        act = _ffn_up(hbf, ffn_w_up, l, ffn_conv_w[l], ffn_conv_b[l], s_len)
        pre = _matmul_residual(act, w_down_bf, l, h, alpha, DOWN_TM, DOWN_TN, "ffn_down")
        h, hbf = _layernorm(pre, ln2_w[l], ln2_b[l])
    return h.reshape(bsz, s_len, d_model)
```

```python
import functools
import math

import jax
import jax.numpy as jnp
from jax import lax
from jax.experimental import pallas as pl
from jax.experimental.pallas import tpu as pltpu

F32 = jnp.float32
BF16 = jnp.bfloat16

HEAD_DIM = 128
RG_C = 8.0
C_GROUP = 4
ROPE_THETA = 10000.0
ROPE_HALF = HEAD_DIM // 2
GRID_W = 64
LN_EPS = 1e-5
RMS_EPS = 1e-6
ATTN_SCALE = HEAD_DIM ** -0.5
LOG2_E = math.log2(math.e)

V7X_VMEM_BYTES = 64 * 1024 * 1024
LANES = 128
SUBLANES = 8
BF16_ROWS = 16

HG_CHUNK = 256
HG_HEADS_PER_STEP = 4
LN_ROWS = 256
MM_TM = 1024
MM_TN = 1024
OUT_TM = 1024
FFN_TM = 1024
FFN_TN = 256
DOWN_TM = 512
DOWN_TN = 512
ATTN_TQ = 2048
ATTN_KB = 1024
PREP_ROWS = 512
RG_ROWS = 512


def _vmem_limit(nbytes):
    return int(min(V7X_VMEM_BYTES - 4 * 1024 * 1024, max(nbytes, 16 * 1024 * 1024)))


def _params(sem, vmem_bytes):
    return pltpu.CompilerParams(dimension_semantics=sem, vmem_limit_bytes=_vmem_limit(vmem_bytes))


def _dot(a, b):
    return jnp.dot(a, b, preferred_element_type=F32)


def _dot_nt(a, b):
    return lax.dot_general(a, b, (((1,), (1,)), ((), ())), preferred_element_type=F32)


def _dot_tn(a, b):
    return lax.dot_general(a, b, (((0,), (0,)), ((), ())), preferred_element_type=F32)


def _sigmoid(x):
    return 1.0 / (1.0 + jnp.exp(-x))


def _ln_kernel(x_ref, w_ref, b_ref, o_ref, obf_ref):
    x = x_ref[...]
    mu = jnp.mean(x, axis=-1, keepdims=True)
    xc = x - mu
    var = jnp.mean(xc * xc, axis=-1, keepdims=True)
    y = xc * lax.rsqrt(var + LN_EPS) * w_ref[...] + b_ref[...]
    o_ref[...] = y
    obf_ref[...] = y.astype(BF16)


def _layernorm(x, w, b):
    t, d = x.shape
    tr = min(LN_ROWS, t)
    return pl.pallas_call(
        _ln_kernel,
        grid=(t // tr,),
        in_specs=[pl.BlockSpec((tr, d), lambda i: (i, 0)),
                  pl.BlockSpec((1, d), lambda i: (0, 0)),
                  pl.BlockSpec((1, d), lambda i: (0, 0))],
        out_specs=[pl.BlockSpec((tr, d), lambda i: (i, 0)),
                   pl.BlockSpec((tr, d), lambda i: (i, 0))],
        out_shape=[jax.ShapeDtypeStruct((t, d), F32), jax.ShapeDtypeStruct((t, d), BF16)],
        compiler_params=_params(("parallel",), 2 * tr * d * 10 + (8 << 20)),
        name="layernorm",
    )(x, w.reshape(1, d), b.reshape(1, d))


def _mm_kernel(x_ref, w_ref, o_ref):
    o_ref[...] = _dot(x_ref[...], w_ref[...].astype(BF16)).astype(o_ref.dtype)


def _matmul(x, w, layer, tm, tn, out_dtype, name, n_cols=None):
    m, k = x.shape
    n = w.shape[2] if n_cols is None else n_cols
    tm, tn = min(tm, m), min(tn, n)
    vm = 2 * (tm * k * 2 + k * tn * w.dtype.itemsize + tm * tn * 4) + tm * tn * 4 + k * tn * 2 + (4 << 20)
    return pl.pallas_call(
        _mm_kernel,
        grid=(m // tm, n // tn),
        in_specs=[pl.BlockSpec((tm, k), lambda i, j: (i, 0)),
                  pl.BlockSpec((None, k, tn), lambda i, j: (layer, 0, j))],
        out_specs=pl.BlockSpec((tm, tn), lambda i, j: (i, j)),
        out_shape=jax.ShapeDtypeStruct((m, n), out_dtype),
        compiler_params=_params(("parallel", "arbitrary"), vm),
        name=name,
    )(x, w)


def _mm_res_kernel(x_ref, w_ref, r_ref, o_ref, *, alpha):
    o_ref[...] = alpha * r_ref[...] + _dot(x_ref[...], w_ref[...])


def _matmul_residual(x, w, layer, res, alpha, tm, tn, name):
    m, k = x.shape
    n = w.shape[2]
    tm, tn = min(tm, m), min(tn, n)
    vm = 2 * (tm * k * 2 + k * tn * 2 + 2 * tm * tn * 4) + tm * tn * 4 + (4 << 20)
    return pl.pallas_call(
        functools.partial(_mm_res_kernel, alpha=alpha),
        grid=(m // tm, n // tn),
        in_specs=[pl.BlockSpec((tm, k), lambda i, j: (i, 0)),
                  pl.BlockSpec((None, k, tn), lambda i, j: (layer, 0, j)),
                  pl.BlockSpec((tm, tn), lambda i, j: (i, j))],
        out_specs=pl.BlockSpec((tm, tn), lambda i, j: (i, j)),
        out_shape=jax.ShapeDtypeStruct((m, n), F32),
        compiler_params=_params(("parallel", "arbitrary"), vm),
        name=name,
    )(x, w, res)


def _hgrn_chunk(q, z, v, lb, st_ref, rev, row, pair_code):
    c_len = q.shape[0]
    n_lvl = c_len.bit_length() - 1
    t = jnp.exp(-jnp.abs(z))
    r = 1.0 / (1.0 + t)
    tr = t * r
    pos = z >= 0.0
    f = lb + (1.0 - lb) * jnp.where(pos, r, tr)
    kk = (1.0 - lb) * jnp.where(pos, tr, r)

    vb = v.astype(BF16)
    scores = _dot_nt(q.astype(BF16), kk.astype(BF16))
    ep = f
    ex = jnp.ones_like(f)
    et = f
    for lvl in range(n_lvl):
        m = 1 << lvl
        later = ((row & m) == 0) if rev else ((row & m) != 0)
        x = jnp.where(later, q * ep, kk * ex).astype(BF16)
        scores = jnp.where(pair_code >= m, _dot_nt(x, x), scores)
        from_earlier, from_later = (c_len - m, m) if rev else (m, c_len - m)
        sib = jnp.where(later, pltpu.roll(et, from_earlier, 0), pltpu.roll(et, from_later, 0))
        ep = jnp.where(later, ep * sib, ep)
        ex = jnp.where(later, ex, ex * sib)
        et = et * sib
    scores = jnp.where(pair_code < 0, 0.0, scores)

    st = st_ref[...]
    o = _dot_nt((q * ep).astype(BF16), st.astype(BF16)) + _dot(scores.astype(BF16), vb)
    st_ref[...] = et[0:1, :] * st + _dot_tn(vb, (kk * ex).astype(BF16))
    return o


def _hgrn_kernel(q_ref, zf_ref, zb_ref, v_ref, g_ref, lb_ref, nw_ref, o_ref,
                 of_scr, ob_scr, stf_scr, stb_scr, *, chunk):
    s_len = q_ref.shape[0]
    n_chunks = s_len // chunk
    row = lax.broadcasted_iota(jnp.int32, (chunk, HEAD_DIM), 0)
    ri = lax.broadcasted_iota(jnp.int32, (chunk, chunk), 0)
    ci = lax.broadcasted_iota(jnp.int32, (chunk, chunk), 1)
    code_f = jnp.where(ri >= ci, ri ^ ci, -1)
    code_b = jnp.where(ci >= ri, ri ^ ci, -1)
    n_heads = q_ref.shape[1] // HEAD_DIM
    stf_scr[...] = jnp.zeros_like(stf_scr)
    stb_scr[...] = jnp.zeros_like(stb_scr)

    def body(i, carry):
        rf = pl.multiple_of(i * chunk, chunk)
        rb = pl.multiple_of((n_chunks - 1 - i) * chunk, chunk)
        for hd in range(n_heads):
            sl = slice(hd * HEAD_DIM, (hd + 1) * HEAD_DIM)
            of_scr[pl.ds(rf, chunk), sl] = _hgrn_chunk(
                q_ref[pl.ds(rf, chunk), sl], zf_ref[pl.ds(rf, chunk), sl], v_ref[pl.ds(rf, chunk), sl],
                lb_ref[0:1, sl], stf_scr.at[hd], False, row, code_f)
            ob_scr[pl.ds(rb, chunk), sl] = _hgrn_chunk(
                q_ref[pl.ds(rb, chunk), sl], zb_ref[pl.ds(rb, chunk), sl], v_ref[pl.ds(rb, chunk), sl],
                lb_ref[1:2, sl], stb_scr.at[hd], True, row, code_b)
        return carry

    lax.fori_loop(0, n_chunks, body, 0)

    def epilogue(i, carry):
        r = pl.multiple_of(i * chunk, chunk)
        for hd in range(n_heads):
            sl = slice(hd * HEAD_DIM, (hd + 1) * HEAD_DIM)
            o = of_scr[pl.ds(r, chunk), sl] + ob_scr[pl.ds(r, chunk), sl]
            y = o * lax.rsqrt(jnp.mean(o * o, axis=-1, keepdims=True) + RMS_EPS) * nw_ref[:, sl]
            g = g_ref[pl.ds(r, chunk), sl]
            o_ref[pl.ds(r, chunk), sl] = (y * (g * _sigmoid(g))).astype(o_ref.dtype)
        return carry

    lax.fori_loop(0, n_chunks, epilogue, 0)


def _hgrn2(u3, lb, norm_w, a_w):
    bsz, s_len, _ = u3.shape
    heads = a_w // HEAD_DIM
    chunk = min(HG_CHUNK, s_len)
    hps = HG_HEADS_PER_STEP
    wid = hps * HEAD_DIM
    groups = heads // hps

    def col(k):
        return pl.BlockSpec((None, s_len, wid), lambda b, h: (b, 0, k * groups + h))

    out = pl.pallas_call(
        functools.partial(_hgrn_kernel, chunk=chunk),
        grid=(bsz, groups),
        in_specs=[col(0), col(1), col(2), col(3), col(4),
                  pl.BlockSpec((2, wid), lambda b, h: (0, h)),
                  pl.BlockSpec((1, wid), lambda b, h: (0, h))],
        out_specs=pl.BlockSpec((None, s_len, wid), lambda b, h: (b, 0, h)),
        out_shape=jax.ShapeDtypeStruct((bsz, s_len, a_w), BF16),
        scratch_shapes=[pltpu.VMEM((s_len, wid), F32), pltpu.VMEM((s_len, wid), F32),
                        pltpu.VMEM((hps, HEAD_DIM, HEAD_DIM), F32), pltpu.VMEM((hps, HEAD_DIM, HEAD_DIM), F32)],
        compiler_params=_params(("parallel", "arbitrary"), 14 * s_len * wid * 4 + (16 << 20)),
        name="hgrn2",
    )(u3, u3, u3, u3, u3, lb, norm_w.reshape(1, a_w))
    return out.reshape(bsz * s_len, a_w)


def _gelu_tanh(x):
    return 0.5 * x * (1.0 + jnp.tanh(math.sqrt(2.0 / math.pi) * (x + 0.044715 * (x * x * x))))


def _rglru_kernel(x_ref, xp_ref, xn_ref, cw_ref, cb_ref, w_ref, bias_ref, lam_ref, *rest, rev):
    if rev:
        gate_ref, hf_ref, o_ref, carry_scr = rest
    else:
        o_ref, carry_scr = rest
    ts, width = x_ref.shape
    n_grp = ts // SUBLANES
    step = pl.program_id(1)
    tile = pl.num_programs(1) - 1 - step if rev else step

    @pl.when(step == 0)
    def _():
        carry_scr[...] = jnp.zeros_like(carry_scr)

    has_prev = (tile > 0).astype(F32)
    has_next = (tile < pl.num_programs(1) - 1).astype(F32)
    x = x_ref[...]
    row = lax.broadcasted_iota(jnp.int32, (ts, width), 0)
    p6 = xp_ref[SUBLANES - 2:SUBLANES - 1, :] * has_prev
    p7 = xp_ref[SUBLANES - 1:SUBLANES, :] * has_prev
    n0 = xn_ref[0:1, :] * has_next
    xm1 = jnp.where(row == 0, p7, pltpu.roll(x, 1, 0))
    xm2 = jnp.where(row == 0, p6, jnp.where(row == 1, p7, pltpu.roll(x, 2, 0)))
    xp1 = jnp.where(row == ts - 1, n0, pltpu.roll(x, ts - 1, 0))
    xc = (cb_ref[...] + cw_ref[0:1, :] * xm2 + cw_ref[1:2, :] * xm1
          + cw_ref[2:3, :] * x + cw_ref[3:4, :] * xp1)
    lam = lam_ref[...]
    sp = jnp.maximum(-lam, 0.0) + jnp.log1p(jnp.exp(-jnp.abs(lam)))
    sub = lax.broadcasted_iota(jnp.int32, (ts, HEAD_DIM), 0) & (SUBLANES - 1)
    for n in range(width // HEAD_DIM):
        sl = slice(n * HEAD_DIM, (n + 1) * HEAD_DIM)
        xn_blk = xc[:, sl]
        zz = _dot(xn_blk.astype(BF16), w_ref[n])
        r = _sigmoid(zz[:, :HEAD_DIM] + bias_ref[0:1, sl])
        ig = _sigmoid(zz[:, HEAD_DIM:] + bias_ref[1:2, sl])
        log_a = (-RG_C) * r * sp[:, sl]
        a = jnp.exp(log_a)
        b = jnp.sqrt(-jnp.tanh(log_a) * (a * a + 1.0)) * (ig * xn_blk)
        for d in (1, 2, 4):
            if rev:
                a_s, b_s, ok = pltpu.roll(a, ts - d, 0), pltpu.roll(b, ts - d, 0), sub < SUBLANES - d
            else:
                a_s, b_s, ok = pltpu.roll(a, d, 0), pltpu.roll(b, d, 0), sub >= d
            b = jnp.where(ok, a * b_s + b, b)
            a = jnp.where(ok, a * a_s, a)
        h_in = jnp.broadcast_to(carry_scr[:, sl], (SUBLANES, HEAD_DIM))
        last = 0 if rev else SUBLANES - 1
        for g in (range(n_grp - 1, -1, -1) if rev else range(n_grp)):
            rows = slice(g * SUBLANES, (g + 1) * SUBLANES)
            h = a[rows, :] * h_in + b[rows, :]
            h_in = jnp.broadcast_to(h[last:last + 1, :], (SUBLANES, HEAD_DIM))
            if rev:
                o_ref[rows, sl] = (_gelu_tanh(gate_ref[rows, sl]) * (hf_ref[rows, sl] + h)).astype(o_ref.dtype)
            else:
                o_ref[rows, sl] = h
        carry_scr[:, sl] = h_in[0:1, :]


def _rglru_dir(u3, col0, conv_w, conv_b, w_dir, bias_dir, lam_dir, b_w, hf=None):
    bsz, s_len, _ = u3.shape
    rev = hf is not None
    ts = min(RG_ROWS, s_len)
    nt = s_len // ts
    cb = col0 // b_w
    rows8 = ts // SUBLANES

    def tile(i):
        return nt - 1 - i if rev else i

    in_specs = [pl.BlockSpec((None, ts, b_w), lambda b, i: (b, tile(i), cb)),
                pl.BlockSpec((None, SUBLANES, b_w), lambda b, i: (b, jnp.maximum(tile(i) * rows8 - 1, 0), cb)),
                pl.BlockSpec((None, SUBLANES, b_w),
                             lambda b, i: (b, jnp.minimum((tile(i) + 1) * rows8, s_len // SUBLANES - 1), cb)),
                pl.BlockSpec((4, b_w), lambda b, i: (0, 0)),
                pl.BlockSpec((1, b_w), lambda b, i: (0, 0)),
                pl.BlockSpec(w_dir.shape, lambda b, i: (0, 0, 0)),
                pl.BlockSpec((2, b_w), lambda b, i: (0, 0)),
                pl.BlockSpec((1, b_w), lambda b, i: (0, 0))]
    args = [u3, u3, u3, conv_w, conv_b.reshape(1, b_w), w_dir, bias_dir, lam_dir.reshape(1, b_w)]
    if rev:
        in_specs += [pl.BlockSpec((None, ts, b_w), lambda b, i: (b, tile(i), cb + 1)),
                     pl.BlockSpec((None, ts, b_w), lambda b, i: (b, tile(i), 0))]
        args += [u3, hf]
    return pl.pallas_call(
        functools.partial(_rglru_kernel, rev=rev),
        grid=(bsz, nt),
        in_specs=in_specs,
        out_specs=pl.BlockSpec((None, ts, b_w), lambda b, i: (b, tile(i), 0)),
        out_shape=jax.ShapeDtypeStruct((bsz, s_len, b_w), BF16 if rev else F32),
        scratch_shapes=[pltpu.VMEM((1, b_w), F32)],
        compiler_params=_params(("parallel", "arbitrary"), 2 * 4 * ts * b_w * 4 + (24 << 20)),
        name="rglru_bwd" if rev else "rglru_fwd",
    )(*args)


def _attn_prep_kernel(q0_ref, q1_ref, kv_ref, cos_ref, sa_ref, sb_ref, qn_ref, kn_ref,
                      qo_ref, ko_ref, vo_ref):
    cos = cos_ref[...]
    sin_a = sa_ref[...]
    sin_b = sb_ref[...]

    def norm_rope(t, w):
        y = t * lax.rsqrt(jnp.mean(t * t, axis=-1, keepdims=True) + RMS_EPS) * w
        return (y * cos + pltpu.roll(y, HEAD_DIM - ROPE_HALF // 2, 1) * sin_a
                + pltpu.roll(y, ROPE_HALF // 2, 1) * sin_b)

    half_heads = q0_ref.shape[1] // HEAD_DIM
    kv_heads = ko_ref.shape[1] // HEAD_DIM
    qn = qn_ref[...]
    kn = kn_ref[...]
    for hd in range(half_heads):
        sl = slice(hd * HEAD_DIM, (hd + 1) * HEAD_DIM)
        so = slice((half_heads + hd) * HEAD_DIM, (half_heads + hd + 1) * HEAD_DIM)
        qo_ref[:, sl] = norm_rope(q0_ref[:, sl], qn).astype(BF16)
        qo_ref[:, so] = norm_rope(q1_ref[:, sl], qn).astype(BF16)
    for hd in range(kv_heads):
        sl = slice(hd * HEAD_DIM, (hd + 1) * HEAD_DIM)
        ko_ref[:, sl] = norm_rope(kv_ref[:, sl], kn).astype(BF16)
    vo_ref[...] = kv_ref[:, kv_heads * HEAD_DIM:].T.astype(BF16)


def _attn_prep(u2, col0, c_w, kv_w, s_len, cos, sin_a, sin_b, qn_w, kn_w):
    t = u2.shape[0]
    tr = min(PREP_ROWS, s_len)
    half = c_w // 2
    assert col0 % half == 0 and 2 * kv_w == half
    cb = col0 // half
    npos = s_len // tr
    tab = pl.BlockSpec((tr, HEAD_DIM), lambda i: (i % npos, 0))
    vec = pl.BlockSpec((1, HEAD_DIM), lambda i: (0, 0))
    return pl.pallas_call(
        _attn_prep_kernel,
        grid=(t // tr,),
        in_specs=[pl.BlockSpec((tr, half), lambda i: (i, cb)),
                  pl.BlockSpec((tr, half), lambda i: (i, cb + 1)),
                  pl.BlockSpec((tr, half), lambda i: (i, cb + 2)),
                  tab, tab, tab, vec, vec],
        out_specs=[pl.BlockSpec((tr, c_w), lambda i: (i, 0)),
                   pl.BlockSpec((tr, kv_w), lambda i: (i, 0)),
                   pl.BlockSpec((kv_w, tr), lambda i: (0, i))],
        out_shape=[jax.ShapeDtypeStruct((t, c_w), BF16),
                   jax.ShapeDtypeStruct((t, kv_w), BF16),
                   jax.ShapeDtypeStruct((kv_w, t), BF16)],
        compiler_params=_params(("parallel",), 2 * tr * (3 * half * 4 + (c_w + 2 * kv_w) * 2) + (16 << 20)),
        name="attn_prep",
    )(u2, u2, u2, cos, sin_a, sin_b, qn_w.reshape(1, HEAD_DIM), kn_w.reshape(1, HEAD_DIM))


def _attn_kernel(q_ref, k_ref, vt_ref, o_ref, sc_scr):
    k = k_ref[...]
    vt = vt_ref[...]
    vt_aug = jnp.concatenate([vt, jnp.ones((BF16_ROWS, vt.shape[1]), BF16)], axis=0)
    s_len = k.shape[0]
    kc = min(ATTN_KB, s_len)
    n_kc = s_len // kc

    def scores(g, c, run_max):
        rows = slice(c * kc, (c + 1) * kc)
        sc = _dot_nt(k[rows, :], q_ref[:, g * HEAD_DIM:(g + 1) * HEAD_DIM])
        sc_scr[g % 2, rows, :] = sc
        cmax = jnp.max(sc.reshape(kc // SUBLANES, SUBLANES, sc.shape[1]), axis=0)
        return cmax if run_max is None else jnp.maximum(run_max, cmax)

    next_max = None
    for c in range(n_kc):
        next_max = scores(0, c, next_max)
    for g in range(C_GROUP):
        mx = jnp.max(next_max, axis=0, keepdims=True)
        next_max = None
        acc = None
        for c in range(n_kc):
            rows = slice(c * kc, (c + 1) * kc)
            p = jnp.exp2((sc_scr[g % 2, rows, :] - mx) * (ATTN_SCALE * LOG2_E)).astype(BF16)
            if g + 1 < C_GROUP:
                next_max = scores(g + 1, c, next_max)
            part = _dot(vt_aug[:, rows], p)
            acc = part if acc is None else acc + part
        ot = acc[:HEAD_DIM, :] / acc[HEAD_DIM:HEAD_DIM + 1, :]
        o_ref[:, g * HEAD_DIM:(g + 1) * HEAD_DIM] = ot.T.astype(o_ref.dtype)


def _attention(qr, kr, vt, bsz, s_len):
    t, c_w = qr.shape
    kv_heads = kr.shape[1] // HEAD_DIM
    gw = C_GROUP * HEAD_DIM
    tq = min(ATTN_TQ, s_len)
    nq = s_len // tq
    return pl.pallas_call(
        _attn_kernel,
        grid=(bsz, kv_heads, nq),
        in_specs=[pl.BlockSpec((tq, gw), lambda b, h, i: (b * nq + i, h)),
                  pl.BlockSpec((s_len, HEAD_DIM), lambda b, h, i: (b, h)),
                  pl.BlockSpec((HEAD_DIM, s_len), lambda b, h, i: (h, b))],
        out_specs=pl.BlockSpec((tq, gw), lambda b, h, i: (b * nq + i, h)),
        out_shape=jax.ShapeDtypeStruct((t, c_w), BF16),
        scratch_shapes=[pltpu.VMEM((2, s_len, tq), F32)],
        compiler_params=_params(("parallel", "parallel", "arbitrary"),
                                8 * tq * s_len * 4 + 4 * s_len * HEAD_DIM * 2 + (8 << 20)),
        name="attention",
    )(qr, kr, vt)


def _outproj_kernel(ya_ref, yb_ref, yc_ref, wa_ref, wb_ref, wc_ref, res_ref, o_ref, *, alpha):
    acc = _dot(ya_ref[...], wa_ref[...])
    acc = acc + _dot(yb_ref[...], wb_ref[...])
    acc = acc + _dot(yc_ref[...], wc_ref[...])
    o_ref[...] = alpha * res_ref[...] + acc


def _outproj(ya, yb, yc, w_out, layer, res, alpha):
    t, a_w = ya.shape
    b_w = yb.shape[1]
    c_w = yc.shape[1]
    n = w_out.shape[2]
    assert a_w == b_w and (a_w + b_w) % c_w == 0
    tm = min(OUT_TM, t)
    tn = min(MM_TN, n)
    c_blk = (a_w + b_w) // c_w
    vm = 2 * (tm * (a_w + b_w + c_w) * 2 + (a_w + b_w + c_w) * tn * 2 + 2 * tm * tn * 4) + tm * tn * 4 + (4 << 20)
    return pl.pallas_call(
        functools.partial(_outproj_kernel, alpha=alpha),
        grid=(t // tm, n // tn),
        in_specs=[pl.BlockSpec((tm, a_w), lambda i, j: (i, 0)),
                  pl.BlockSpec((tm, b_w), lambda i, j: (i, 0)),
                  pl.BlockSpec((tm, c_w), lambda i, j: (i, 0)),
                  pl.BlockSpec((None, a_w, tn), lambda i, j: (layer, 0, j)),
                  pl.BlockSpec((None, b_w, tn), lambda i, j: (layer, 1, j)),
                  pl.BlockSpec((None, c_w, tn), lambda i, j: (layer, c_blk, j)),
                  pl.BlockSpec((tm, tn), lambda i, j: (i, j))],
        out_specs=pl.BlockSpec((tm, tn), lambda i, j: (i, j)),
        out_shape=jax.ShapeDtypeStruct((t, n), F32),
        compiler_params=_params(("parallel", "arbitrary"), vm),
        name="outproj",
    )(ya, yb, yc, w_out, w_out, w_out, res)


def _ffn_up_kernel(x_ref, wg_ref, wu_ref, gp_ref, gn_ref, cw_ref, cb_ref, o_ref, wg_scr, wu_scr,
                   *, tiles_per_seq):
    tm, tn = o_ref.shape

    @pl.when(pl.program_id(1) == 0)
    def _():
        wg_scr[...] = wg_ref[...].astype(BF16)
        wu_scr[...] = wu_ref[...].astype(BF16)

    x = x_ref[...]
    g = _dot(x, wg_scr[...])
    up = _dot(x, wu_scr[...])
    pos = pl.program_id(1) % tiles_per_seq
    has_prev = (pos > 0).astype(F32)
    has_next = (pos < tiles_per_seq - 1).astype(F32)
    g_prev = gp_ref[0, 1:2, :] * has_prev
    g_next = gn_ref[0, 0:1, :] * has_next
    row = lax.broadcasted_iota(jnp.int32, (tm, tn), 0)
    g_m1 = jnp.where(row == 0, g_prev, pltpu.roll(g, 1, 0))
    g_p1 = jnp.where(row == tm - 1, g_next, pltpu.roll(g, tm - 1, 0))
    y = cb_ref[...] + cw_ref[0:1, :] * g_m1 + cw_ref[1:2, :] * g + cw_ref[2:3, :] * g_p1
    o_ref[...] = ((y * _sigmoid(y)) * up).astype(o_ref.dtype)


def _ffn_up(hbf, w_up, layer, conv_w, conv_b, s_len):
    t, d = hbf.shape
    d_ff = w_up.shape[2] // 2
    tm = min(FFN_TM, s_len)
    tn = FFN_TN
    assert d_ff % tn == 0
    nj = d_ff // tn
    n_tiles = t // tm
    tiles_per_seq = s_len // tm
    edge_x = hbf.reshape(n_tiles, tm, d)[:, (0, tm - 1), :].reshape(2 * n_tiles, d)
    g_edge = _matmul(edge_x, w_up, layer, 2 * n_tiles, tn, F32, "ffn_edge", n_cols=d_ff)
    g_edge = g_edge.reshape(n_tiles, 2, d_ff)
    vm = 2 * (tm * d * 2 + 2 * d * tn * 4 + tm * tn * 2) + 2 * d * tn * 2 + 8 * tm * tn * 4 + (4 << 20)
    w_scr = pltpu.VMEM((d, tn), BF16)
    return pl.pallas_call(
        functools.partial(_ffn_up_kernel, tiles_per_seq=tiles_per_seq),
        grid=(nj, n_tiles),
        in_specs=[pl.BlockSpec((tm, d), lambda j, i: (i, 0)),
                  pl.BlockSpec((None, d, tn), lambda j, i: (layer, 0, j)),
                  pl.BlockSpec((None, d, tn), lambda j, i: (layer, 0, nj + j)),
                  pl.BlockSpec((1, 2, tn), lambda j, i: (jnp.maximum(i - 1, 0), 0, j)),
                  pl.BlockSpec((1, 2, tn), lambda j, i: (jnp.minimum(i + 1, n_tiles - 1), 0, j)),
                  pl.BlockSpec((3, tn), lambda j, i: (0, j)),
                  pl.BlockSpec((1, tn), lambda j, i: (0, j))],
        out_specs=pl.BlockSpec((tm, tn), lambda j, i: (i, j)),
        out_shape=jax.ShapeDtypeStruct((t, d_ff), BF16),
        scratch_shapes=[w_scr, w_scr],
        compiler_params=_params(("parallel", "arbitrary"), vm),
        name="ffn_up",
    )(hbf, w_up, w_up, g_edge, g_edge, conv_w, conv_b.reshape(1, d_ff))


def _rope_tables(s_len):
    rows = s_len // GRID_W
    g_r, g_c = jnp.meshgrid(jnp.arange(rows), jnp.arange(GRID_W), indexing='ij')
    row = g_r.reshape(s_len).astype(F32)
    colp = g_c.reshape(s_len).astype(F32)
    inv_freq = ROPE_THETA ** (-jnp.arange(0, ROPE_HALF, 2, dtype=F32) / ROPE_HALF)
    ang_r = row[:, None] * inv_freq[None, :]
    ang_c = colp[:, None] * inv_freq[None, :]
    ang = jnp.concatenate([ang_r, ang_r, ang_c, ang_c], axis=-1)
    cos, sin = jnp.cos(ang), jnp.sin(ang)
    first_quarter = (jnp.arange(HEAD_DIM) & (ROPE_HALF // 2)) == 0
    sin_a = jnp.where(first_quarter, -sin, 0.0)
    sin_b = jnp.where(first_quarter, 0.0, sin)
    return cos, sin_a, sin_b


def kernel(x, emb_ln_w, emb_ln_b, w_in, hgrn_lb_logits, hgrn_norm_w, rglru_conv_w, rglru_conv_b,
           rglru_wa, rglru_ba, rglru_wx, rglru_bx, rglru_lambda, attn_q_norm_w, attn_k_norm_w,
           w_out, ln1_w, ln1_b, ffn_w_up, ffn_conv_w, ffn_conv_b, ffn_w_down, ln2_w, ln2_b):
    bsz, s_len, d_model = x.shape
    depth = w_in.shape[0]
    t = bsz * s_len
    a_w = hgrn_norm_w.shape[1]
    b_w = rglru_conv_w.shape[2]
    in_cols = w_in.shape[2]
    a_cols = 5 * a_w
    b_cols = 2 * b_w
    c_w = w_out.shape[1] - a_w - b_w
    kv_w = (in_cols - a_cols - b_cols - c_w) // 2
    alpha = (2.0 * depth) ** 0.25

    cos, sin_a, sin_b = _rope_tables(s_len)
    lb_cs = jnp.cumsum(jax.nn.softmax(hgrn_lb_logits.astype(F32), axis=0), axis=0)
    lower_bounds = lb_cs - lb_cs[0:1]

    w_in_bf = w_in.astype(BF16)
    w_out_bf = w_out.astype(BF16)
    w_down_bf = ffn_w_down.astype(BF16)

    h, hbf = _layernorm(x.reshape(t, d_model), emb_ln_w, emb_ln_b)
    for l in range(depth):
        u2 = _matmul(hbf, w_in_bf, l, MM_TM, MM_TN, F32, "in_proj")
        u3 = u2.reshape(bsz, s_len, in_cols)

        ya = _hgrn2(u3, lower_bounds[l], hgrn_norm_w[l], a_w)

        yb = None
        for d in range(2):
            w_dir = jnp.concatenate([rglru_wa[l, d], rglru_wx[l, d]], axis=-1).astype(BF16)
            bias_dir = jnp.stack([rglru_ba[l, d], rglru_bx[l, d]])
            yb = _rglru_dir(u3, a_cols, rglru_conv_w[l], rglru_conv_b[l], w_dir, bias_dir,
                            rglru_lambda[l, d], b_w, hf=yb)
        yb = yb.reshape(t, b_w)

        qr, kr, vt = _attn_prep(u2, a_cols + b_cols, c_w, kv_w, s_len, cos, sin_a, sin_b,
                                attn_q_norm_w[l], attn_k_norm_w[l])
        yc = _attention(qr, kr, vt, bsz, s_len)

        pre = _outproj(ya, yb, yc, w_out_bf, l, h, alpha)
        h, hbf = _layernorm(pre, ln1_w[l], ln1_b[l])

        act = _ffn_up(hbf, ffn_w_up, l, ffn_conv_w[l], ffn_conv_b[l], s_len)
        pre = _matmul_residual(act, w_down_bf, l, h, alpha, DOWN_TM, DOWN_TN, "ffn_down")
        h, hbf = _layernorm(pre, ln2_w[l], ln2_b[l])
    return h.reshape(bsz, s_len, d_model)
```

```python
import functools
import math

import jax
import jax.numpy as jnp
from jax import lax
from jax.experimental import pallas as pl
from jax.experimental.pallas import tpu as pltpu

F32 = jnp.float32
BF16 = jnp.bfloat16

HEAD_DIM = 128
RG_C = 8.0
C_GROUP = 4
ROPE_THETA = 10000.0
ROPE_HALF = HEAD_DIM // 2
GRID_W = 64
LN_EPS = 1e-5
RMS_EPS = 1e-6
ATTN_SCALE = HEAD_DIM ** -0.5
LOG2_E = math.log2(math.e)

V7X_VMEM_BYTES = 64 * 1024 * 1024
LANES = 128
SUBLANES = 8
BF16_ROWS = 16

HG_CHUNK = 256
HG_HEADS_PER_STEP = 4
LN_ROWS = 256
MM_TM = 1024
MM_TN = 1024
OUT_TM = 1024
FFN_TM = 1024
FFN_TN = 256
DOWN_TM = 512
DOWN_TN = 512
ATTN_TQ = 2048
ATTN_KB = 1024
PREP_ROWS = 512
RG_ROWS = 512


def _vmem_limit(nbytes):
    return int(min(V7X_VMEM_BYTES - 4 * 1024 * 1024, max(nbytes, 16 * 1024 * 1024)))


def _params(sem, vmem_bytes):
    return pltpu.CompilerParams(dimension_semantics=sem, vmem_limit_bytes=_vmem_limit(vmem_bytes))


def _dot(a, b):
    return jnp.dot(a, b, preferred_element_type=F32)


def _dot_nt(a, b):
    return lax.dot_general(a, b, (((1,), (1,)), ((), ())), preferred_element_type=F32)


def _dot_tn(a, b):
    return lax.dot_general(a, b, (((0,), (0,)), ((), ())), preferred_element_type=F32)


def _sigmoid(x):
    return 1.0 / (1.0 + jnp.exp(-x))


def _ln_kernel(x_ref, w_ref, b_ref, o_ref, obf_ref):
    x = x_ref[...]
    mu = jnp.mean(x, axis=-1, keepdims=True)
    xc = x - mu
    var = jnp.mean(xc * xc, axis=-1, keepdims=True)
    y = xc * lax.rsqrt(var + LN_EPS) * w_ref[...] + b_ref[...]
    o_ref[...] = y
    obf_ref[...] = y.astype(BF16)


def _layernorm(x, w, b):
    t, d = x.shape
    tr = min(LN_ROWS, t)
    return pl.pallas_call(
        _ln_kernel,
        grid=(t // tr,),
        in_specs=[pl.BlockSpec((tr, d), lambda i: (i, 0)),
                  pl.BlockSpec((1, d), lambda i: (0, 0)),
                  pl.BlockSpec((1, d), lambda i: (0, 0))],
        out_specs=[pl.BlockSpec((tr, d), lambda i: (i, 0)),
                   pl.BlockSpec((tr, d), lambda i: (i, 0))],
        out_shape=[jax.ShapeDtypeStruct((t, d), F32), jax.ShapeDtypeStruct((t, d), BF16)],
        compiler_params=_params(("parallel",), 2 * tr * d * 10 + (8 << 20)),
        name="layernorm",
    )(x, w.reshape(1, d), b.reshape(1, d))


def _mm_kernel(x_ref, w_ref, o_ref):
    o_ref[...] = _dot(x_ref[...], w_ref[...].astype(BF16)).astype(o_ref.dtype)


def _matmul(x, w, layer, tm, tn, out_dtype, name, n_cols=None):
    m, k = x.shape
    n = w.shape[2] if n_cols is None else n_cols
    tm, tn = min(tm, m), min(tn, n)
    vm = 2 * (tm * k * 2 + k * tn * w.dtype.itemsize + tm * tn * 4) + tm * tn * 4 + k * tn * 2 + (4 << 20)
    return pl.pallas_call(
        _mm_kernel,
        grid=(m // tm, n // tn),
        in_specs=[pl.BlockSpec((tm, k), lambda i, j: (i, 0)),
                  pl.BlockSpec((None, k, tn), lambda i, j: (layer, 0, j))],
        out_specs=pl.BlockSpec((tm, tn), lambda i, j: (i, j)),
        out_shape=jax.ShapeDtypeStruct((m, n), out_dtype),
        compiler_params=_params(("parallel", "arbitrary"), vm),
        name=name,
    )(x, w)


def _mm_res_kernel(x_ref, w_ref, r_ref, o_ref, *, alpha):
    o_ref[...] = alpha * r_ref[...] + _dot(x_ref[...], w_ref[...])


def _matmul_residual(x, w, layer, res, alpha, tm, tn, name):
    m, k = x.shape
    n = w.shape[2]
    tm, tn = min(tm, m), min(tn, n)
    vm = 2 * (tm * k * 2 + k * tn * 2 + 2 * tm * tn * 4) + tm * tn * 4 + (4 << 20)
    return pl.pallas_call(
        functools.partial(_mm_res_kernel, alpha=alpha),
        grid=(m // tm, n // tn),
        in_specs=[pl.BlockSpec((tm, k), lambda i, j: (i, 0)),
                  pl.BlockSpec((None, k, tn), lambda i, j: (layer, 0, j)),
                  pl.BlockSpec((tm, tn), lambda i, j: (i, j))],
        out_specs=pl.BlockSpec((tm, tn), lambda i, j: (i, j)),
        out_shape=jax.ShapeDtypeStruct((m, n), F32),
        compiler_params=_params(("parallel", "arbitrary"), vm),
        name=name,
    )(x, w, res)


def _hgrn_chunk(q, z, v, lb, st_ref, rev, row, pair_code):
    c_len = q.shape[0]
    n_lvl = c_len.bit_length() - 1
    t = jnp.exp(-jnp.abs(z))
    r = 1.0 / (1.0 + t)
    tr = t * r
    pos = z >= 0.0
    f = lb + (1.0 - lb) * jnp.where(pos, r, tr)
    kk = (1.0 - lb) * jnp.where(pos, tr, r)

    vb = v.astype(BF16)
    scores = _dot_nt(q.astype(BF16), kk.astype(BF16))
    ep = f
    ex = jnp.ones_like(f)
    et = f
    for lvl in range(n_lvl):
        m = 1 << lvl
        later = ((row & m) == 0) if rev else ((row & m) != 0)
        x = jnp.where(later, q * ep, kk * ex).astype(BF16)
        scores = jnp.where(pair_code >= m, _dot_nt(x, x), scores)
        from_earlier, from_later = (c_len - m, m) if rev else (m, c_len - m)
        sib = jnp.where(later, pltpu.roll(et, from_earlier, 0), pltpu.roll(et, from_later, 0))
        ep = jnp.where(later, ep * sib, ep)
        ex = jnp.where(later, ex, ex * sib)
        et = et * sib
    scores = jnp.where(pair_code < 0, 0.0, scores)

    st = st_ref[...]
    o = _dot_nt((q * ep).astype(BF16), st.astype(BF16)) + _dot(scores.astype(BF16), vb)
    st_ref[...] = et[0:1, :] * st + _dot_tn(vb, (kk * ex).astype(BF16))
    return o


def _hgrn_kernel(q_ref, zf_ref, zb_ref, v_ref, g_ref, lb_ref, nw_ref, o_ref,
                 of_scr, ob_scr, stf_scr, stb_scr, *, chunk):
    s_len = q_ref.shape[0]
    n_chunks = s_len // chunk
    row = lax.broadcasted_iota(jnp.int32, (chunk, HEAD_DIM), 0)
    ri = lax.broadcasted_iota(jnp.int32, (chunk, chunk), 0)
    ci = lax.broadcasted_iota(jnp.int32, (chunk, chunk), 1)
    code_f = jnp.where(ri >= ci, ri ^ ci, -1)
    code_b = jnp.where(ci >= ri, ri ^ ci, -1)
    n_heads = q_ref.shape[1] // HEAD_DIM
    stf_scr[...] = jnp.zeros_like(stf_scr)
    stb_scr[...] = jnp.zeros_like(stb_scr)

    def body(i, carry):
        rf = pl.multiple_of(i * chunk, chunk)
        rb = pl.multiple_of((n_chunks - 1 - i) * chunk, chunk)
        for hd in range(n_heads):
            sl = slice(hd * HEAD_DIM, (hd + 1) * HEAD_DIM)
            of_scr[pl.ds(rf, chunk), sl] = _hgrn_chunk(
                q_ref[pl.ds(rf, chunk), sl], zf_ref[pl.ds(rf, chunk), sl], v_ref[pl.ds(rf, chunk), sl],
                lb_ref[0:1, sl], stf_scr.at[hd], False, row, code_f)
            ob_scr[pl.ds(rb, chunk), sl] = _hgrn_chunk(
                q_ref[pl.ds(rb, chunk), sl], zb_ref[pl.ds(rb, chunk), sl], v_ref[pl.ds(rb, chunk), sl],
                lb_ref[1:2, sl], stb_scr.at[hd], True, row, code_b)
        return carry

    lax.fori_loop(0, n_chunks, body, 0)

    def epilogue(i, carry):
        r = pl.multiple_of(i * chunk, chunk)
        for hd in range(n_heads):
            sl = slice(hd * HEAD_DIM, (hd + 1) * HEAD_DIM)
            o = of_scr[pl.ds(r, chunk), sl] + ob_scr[pl.ds(r, chunk), sl]
            y = o * lax.rsqrt(jnp.mean(o * o, axis=-1, keepdims=True) + RMS_EPS) * nw_ref[:, sl]
            g = g_ref[pl.ds(r, chunk), sl]
            o_ref[pl.ds(r, chunk), sl] = (y * (g * _sigmoid(g))).astype(o_ref.dtype)
        return carry

    lax.fori_loop(0, n_chunks, epilogue, 0)


def _hgrn2(u3, lb, norm_w, a_w):
    bsz, s_len, _ = u3.shape
    heads = a_w // HEAD_DIM
    chunk = min(HG_CHUNK, s_len)
    hps = HG_HEADS_PER_STEP
    wid = hps * HEAD_DIM
    groups = heads // hps

    def col(k):
        return pl.BlockSpec((None, s_len, wid), lambda b, h: (b, 0, k * groups + h))

    out = pl.pallas_call(
        functools.partial(_hgrn_kernel, chunk=chunk),
        grid=(bsz, groups),
        in_specs=[col(0), col(1), col(2), col(3), col(4),
                  pl.BlockSpec((2, wid), lambda b, h: (0, h)),
                  pl.BlockSpec((1, wid), lambda b, h: (0, h))],
        out_specs=pl.BlockSpec((None, s_len, wid), lambda b, h: (b, 0, h)),
        out_shape=jax.ShapeDtypeStruct((bsz, s_len, a_w), BF16),
        scratch_shapes=[pltpu.VMEM((s_len, wid), F32), pltpu.VMEM((s_len, wid), F32),
                        pltpu.VMEM((hps, HEAD_DIM, HEAD_DIM), F32), pltpu.VMEM((hps, HEAD_DIM, HEAD_DIM), F32)],
        compiler_params=_params(("parallel", "arbitrary"), 14 * s_len * wid * 4 + (16 << 20)),
        name="hgrn2",
    )(u3, u3, u3, u3, u3, lb, norm_w.reshape(1, a_w))
    return out.reshape(bsz * s_len, a_w)


def _gelu_tanh(x):
    return 0.5 * x * (1.0 + jnp.tanh(math.sqrt(2.0 / math.pi) * (x + 0.044715 * (x * x * x))))


def _rglru_kernel(x_ref, xp_ref, xn_ref, cw_ref, cb_ref, w_ref, bias_ref, lam_ref, *rest, rev):
    if rev:
        gate_ref, hf_ref, o_ref, carry_scr = rest
    else:
        o_ref, carry_scr = rest
    ts, width = x_ref.shape
    n_grp = ts // SUBLANES
    step = pl.program_id(1)
    tile = pl.num_programs(1) - 1 - step if rev else step

    @pl.when(step == 0)
    def _():
        carry_scr[...] = jnp.zeros_like(carry_scr)

    has_prev = (tile > 0).astype(F32)
    has_next = (tile < pl.num_programs(1) - 1).astype(F32)
    x = x_ref[...]
    row = lax.broadcasted_iota(jnp.int32, (ts, width), 0)
    p6 = xp_ref[SUBLANES - 2:SUBLANES - 1, :] * has_prev
    p7 = xp_ref[SUBLANES - 1:SUBLANES, :] * has_prev
    n0 = xn_ref[0:1, :] * has_next
    xm1 = jnp.where(row == 0, p7, pltpu.roll(x, 1, 0))
    xm2 = jnp.where(row == 0, p6, jnp.where(row == 1, p7, pltpu.roll(x, 2, 0)))
    xp1 = jnp.where(row == ts - 1, n0, pltpu.roll(x, ts - 1, 0))
    xc = (cb_ref[...] + cw_ref[0:1, :] * xm2 + cw_ref[1:2, :] * xm1
          + cw_ref[2:3, :] * x + cw_ref[3:4, :] * xp1)
    lam = lam_ref[...]
    sp = jnp.maximum(-lam, 0.0) + jnp.log1p(jnp.exp(-jnp.abs(lam)))
    sub = lax.broadcasted_iota(jnp.int32, (n_grp, SUBLANES, HEAD_DIM), 1)
    for n in range(width // HEAD_DIM):
        sl = slice(n * HEAD_DIM, (n + 1) * HEAD_DIM)
        xn_blk = xc[:, sl]
        zz = _dot(xn_blk.astype(BF16), w_ref[n])
        r = _sigmoid(zz[:, :HEAD_DIM] + bias_ref[0:1, sl])
        ig = _sigmoid(zz[:, HEAD_DIM:] + bias_ref[1:2, sl])
        log_a = (-RG_C) * r * sp[:, sl]
        a = jnp.exp(log_a)
        b = jnp.sqrt(-jnp.tanh(log_a) * (a * a + 1.0)) * (ig * xn_blk)
        a = a.reshape(n_grp, SUBLANES, HEAD_DIM)
        b = b.reshape(n_grp, SUBLANES, HEAD_DIM)
        for d in (1, 2, 4):
            if rev:
                a_s, b_s, ok = pltpu.roll(a, SUBLANES - d, 1), pltpu.roll(b, SUBLANES - d, 1), sub < SUBLANES - d
            else:
                a_s, b_s, ok = pltpu.roll(a, d, 1), pltpu.roll(b, d, 1), sub >= d
            b = jnp.where(ok, a * b_s + b, b)
            a = jnp.where(ok, a * a_s, a)
        h_in = jnp.broadcast_to(carry_scr[:, sl], (SUBLANES, HEAD_DIM))
        last = 0 if rev else SUBLANES - 1
        for g in (range(n_grp - 1, -1, -1) if rev else range(n_grp)):
            rows = slice(g * SUBLANES, (g + 1) * SUBLANES)
            h = a[g] * h_in + b[g]
            h_in = jnp.broadcast_to(h[last:last + 1, :], (SUBLANES, HEAD_DIM))
            if rev:
                o_ref[rows, sl] = (_gelu_tanh(gate_ref[rows, sl]) * (hf_ref[rows, sl] + h)).astype(o_ref.dtype)
            else:
                o_ref[rows, sl] = h
        carry_scr[:, sl] = h_in[0:1, :]


def _rglru_dir(u3, col0, conv_w, conv_b, w_dir, bias_dir, lam_dir, b_w, hf=None):
    bsz, s_len, _ = u3.shape
    rev = hf is not None
    ts = min(RG_ROWS, s_len)
    nt = s_len // ts
    cb = col0 // b_w
    rows8 = ts // SUBLANES

    def tile(i):
        return nt - 1 - i if rev else i

    in_specs = [pl.BlockSpec((None, ts, b_w), lambda b, i: (b, tile(i), cb)),
                pl.BlockSpec((None, SUBLANES, b_w), lambda b, i: (b, jnp.maximum(tile(i) * rows8 - 1, 0), cb)),
                pl.BlockSpec((None, SUBLANES, b_w),
                             lambda b, i: (b, jnp.minimum((tile(i) + 1) * rows8, s_len // SUBLANES - 1), cb)),
                pl.BlockSpec((4, b_w), lambda b, i: (0, 0)),
                pl.BlockSpec((1, b_w), lambda b, i: (0, 0)),
                pl.BlockSpec(w_dir.shape, lambda b, i: (0, 0, 0)),
                pl.BlockSpec((2, b_w), lambda b, i: (0, 0)),
                pl.BlockSpec((1, b_w), lambda b, i: (0, 0))]
    args = [u3, u3, u3, conv_w, conv_b.reshape(1, b_w), w_dir, bias_dir, lam_dir.reshape(1, b_w)]
    if rev:
        in_specs += [pl.BlockSpec((None, ts, b_w), lambda b, i: (b, tile(i), cb + 1)),
                     pl.BlockSpec((None, ts, b_w), lambda b, i: (b, tile(i), 0))]
        args += [u3, hf]
    return pl.pallas_call(
        functools.partial(_rglru_kernel, rev=rev),
        grid=(bsz, nt),
        in_specs=in_specs,
        out_specs=pl.BlockSpec((None, ts, b_w), lambda b, i: (b, tile(i), 0)),
        out_shape=jax.ShapeDtypeStruct((bsz, s_len, b_w), BF16 if rev else F32),
        scratch_shapes=[pltpu.VMEM((1, b_w), F32)],
        compiler_params=_params(("parallel", "arbitrary"), 2 * 4 * ts * b_w * 4 + (24 << 20)),
        name="rglru_bwd" if rev else "rglru_fwd",
    )(*args)


def _attn_prep_kernel(q0_ref, q1_ref, kv_ref, cos_ref, sa_ref, sb_ref, qn_ref, kn_ref,
                      qo_ref, ko_ref, vo_ref):
    cos = cos_ref[...]
    sin_a = sa_ref[...]
    sin_b = sb_ref[...]

    def norm_rope(t, w):
        y = t * lax.rsqrt(jnp.mean(t * t, axis=-1, keepdims=True) + RMS_EPS) * w
        return (y * cos + pltpu.roll(y, HEAD_DIM - ROPE_HALF // 2, 1) * sin_a
                + pltpu.roll(y, ROPE_HALF // 2, 1) * sin_b)

    half_heads = q0_ref.shape[1] // HEAD_DIM
    kv_heads = ko_ref.shape[1] // HEAD_DIM
    qn = qn_ref[...]
    kn = kn_ref[...]
    for hd in range(half_heads):
        sl = slice(hd * HEAD_DIM, (hd + 1) * HEAD_DIM)
        so = slice((half_heads + hd) * HEAD_DIM, (half_heads + hd + 1) * HEAD_DIM)
        qo_ref[:, sl] = norm_rope(q0_ref[:, sl], qn).astype(BF16)
        qo_ref[:, so] = norm_rope(q1_ref[:, sl], qn).astype(BF16)
    for hd in range(kv_heads):
        sl = slice(hd * HEAD_DIM, (hd + 1) * HEAD_DIM)
        ko_ref[:, sl] = norm_rope(kv_ref[:, sl], kn).astype(BF16)
    vo_ref[...] = kv_ref[:, kv_heads * HEAD_DIM:].T.astype(BF16)


def _attn_prep(u2, col0, c_w, kv_w, s_len, cos, sin_a, sin_b, qn_w, kn_w):
    t = u2.shape[0]
    tr = min(PREP_ROWS, s_len)
    half = c_w // 2
    assert col0 % half == 0 and 2 * kv_w == half
    cb = col0 // half
    npos = s_len // tr
    tab = pl.BlockSpec((tr, HEAD_DIM), lambda i: (i % npos, 0))
    vec = pl.BlockSpec((1, HEAD_DIM), lambda i: (0, 0))
    return pl.pallas_call(
        _attn_prep_kernel,
        grid=(t // tr,),
        in_specs=[pl.BlockSpec((tr, half), lambda i: (i, cb)),
                  pl.BlockSpec((tr, half), lambda i: (i, cb + 1)),
                  pl.BlockSpec((tr, half), lambda i: (i, cb + 2)),
                  tab, tab, tab, vec, vec],
        out_specs=[pl.BlockSpec((tr, c_w), lambda i: (i, 0)),
                   pl.BlockSpec((tr, kv_w), lambda i: (i, 0)),
                   pl.BlockSpec((kv_w, tr), lambda i: (0, i))],
        out_shape=[jax.ShapeDtypeStruct((t, c_w), BF16),
                   jax.ShapeDtypeStruct((t, kv_w), BF16),
                   jax.ShapeDtypeStruct((kv_w, t), BF16)],
        compiler_params=_params(("parallel",), 2 * tr * (3 * half * 4 + (c_w + 2 * kv_w) * 2) + (16 << 20)),
        name="attn_prep",
    )(u2, u2, u2, cos, sin_a, sin_b, qn_w.reshape(1, HEAD_DIM), kn_w.reshape(1, HEAD_DIM))


def _attn_kernel(q_ref, k_ref, vt_ref, o_ref, sc_scr):
    k = k_ref[...]
    vt = vt_ref[...]
    vt_aug = jnp.concatenate([vt, jnp.ones((BF16_ROWS, vt.shape[1]), BF16)], axis=0)
    s_len = k.shape[0]
    kc = min(ATTN_KB, s_len)
    n_kc = s_len // kc

    def scores(g, c, run_max):
        rows = slice(c * kc, (c + 1) * kc)
        sc = _dot_nt(k[rows, :], q_ref[:, g * HEAD_DIM:(g + 1) * HEAD_DIM])
        sc_scr[g % 2, rows, :] = sc
        cmax = jnp.max(sc.reshape(kc // SUBLANES, SUBLANES, sc.shape[1]), axis=0)
        return cmax if run_max is None else jnp.maximum(run_max, cmax)

    next_max = None
    for c in range(n_kc):
        next_max = scores(0, c, next_max)
    for g in range(C_GROUP):
        mx = jnp.max(next_max, axis=0, keepdims=True)
        next_max = None
        acc = None
        for c in range(n_kc):
            rows = slice(c * kc, (c + 1) * kc)
            p = jnp.exp2((sc_scr[g % 2, rows, :] - mx) * (ATTN_SCALE * LOG2_E)).astype(BF16)
            if g + 1 < C_GROUP:
                next_max = scores(g + 1, c, next_max)
            part = _dot(vt_aug[:, rows], p)
            acc = part if acc is None else acc + part
        ot = acc[:HEAD_DIM, :] / acc[HEAD_DIM:HEAD_DIM + 1, :]
        o_ref[:, g * HEAD_DIM:(g + 1) * HEAD_DIM] = ot.T.astype(o_ref.dtype)


def _attention(qr, kr, vt, bsz, s_len):
    t, c_w = qr.shape
    kv_heads = kr.shape[1] // HEAD_DIM
    gw = C_GROUP * HEAD_DIM
    tq = min(ATTN_TQ, s_len)
    nq = s_len // tq
    return pl.pallas_call(
        _attn_kernel,
        grid=(bsz, kv_heads, nq),
        in_specs=[pl.BlockSpec((tq, gw), lambda b, h, i: (b * nq + i, h)),
                  pl.BlockSpec((s_len, HEAD_DIM), lambda b, h, i: (b, h)),
                  pl.BlockSpec((HEAD_DIM, s_len), lambda b, h, i: (h, b))],
        out_specs=pl.BlockSpec((tq, gw), lambda b, h, i: (b * nq + i, h)),
        out_shape=jax.ShapeDtypeStruct((t, c_w), BF16),
        scratch_shapes=[pltpu.VMEM((2, s_len, tq), F32)],
        compiler_params=_params(("parallel", "parallel", "arbitrary"),
                                8 * tq * s_len * 4 + 4 * s_len * HEAD_DIM * 2 + (8 << 20)),
        name="attention",
    )(qr, kr, vt)


def _outproj_kernel(ya_ref, yb_ref, yc_ref, wa_ref, wb_ref, wc_ref, res_ref, o_ref, *, alpha):
    acc = _dot(ya_ref[...], wa_ref[...])
    acc = acc + _dot(yb_ref[...], wb_ref[...])
    acc = acc + _dot(yc_ref[...], wc_ref[...])
    o_ref[...] = alpha * res_ref[...] + acc


def _outproj(ya, yb, yc, w_out, layer, res, alpha):
    t, a_w = ya.shape
    b_w = yb.shape[1]
    c_w = yc.shape[1]
    n = w_out.shape[2]
    assert a_w == b_w and (a_w + b_w) % c_w == 0
    tm = min(OUT_TM, t)
    tn = min(MM_TN, n)
    c_blk = (a_w + b_w) // c_w
    vm = 2 * (tm * (a_w + b_w + c_w) * 2 + (a_w + b_w + c_w) * tn * 2 + 2 * tm * tn * 4) + tm * tn * 4 + (4 << 20)
    return pl.pallas_call(
        functools.partial(_outproj_kernel, alpha=alpha),
        grid=(t // tm, n // tn),
        in_specs=[pl.BlockSpec((tm, a_w), lambda i, j: (i, 0)),
                  pl.BlockSpec((tm, b_w), lambda i, j: (i, 0)),
                  pl.BlockSpec((tm, c_w), lambda i, j: (i, 0)),
                  pl.BlockSpec((None, a_w, tn), lambda i, j: (layer, 0, j)),
                  pl.BlockSpec((None, b_w, tn), lambda i, j: (layer, 1, j)),
                  pl.BlockSpec((None, c_w, tn), lambda i, j: (layer, c_blk, j)),
                  pl.BlockSpec((tm, tn), lambda i, j: (i, j))],
        out_specs=pl.BlockSpec((tm, tn), lambda i, j: (i, j)),
        out_shape=jax.ShapeDtypeStruct((t, n), F32),
        compiler_params=_params(("parallel", "arbitrary"), vm),
        name="outproj",
    )(ya, yb, yc, w_out, w_out, w_out, res)


def _ffn_up_kernel(x_ref, wg_ref, wu_ref, gp_ref, gn_ref, cw_ref, cb_ref, o_ref, wg_scr, wu_scr,
                   *, tiles_per_seq):
    tm, tn = o_ref.shape

    @pl.when(pl.program_id(1) == 0)
    def _():
        wg_scr[...] = wg_ref[...].astype(BF16)
        wu_scr[...] = wu_ref[...].astype(BF16)

    x = x_ref[...]
    g = _dot(x, wg_scr[...])
    up = _dot(x, wu_scr[...])
    pos = pl.program_id(1) % tiles_per_seq
    has_prev = (pos > 0).astype(F32)
    has_next = (pos < tiles_per_seq - 1).astype(F32)
    g_prev = gp_ref[0, 1:2, :] * has_prev
    g_next = gn_ref[0, 0:1, :] * has_next
    row = lax.broadcasted_iota(jnp.int32, (tm, tn), 0)
    g_m1 = jnp.where(row == 0, g_prev, pltpu.roll(g, 1, 0))
    g_p1 = jnp.where(row == tm - 1, g_next, pltpu.roll(g, tm - 1, 0))
    y = cb_ref[...] + cw_ref[0:1, :] * g_m1 + cw_ref[1:2, :] * g + cw_ref[2:3, :] * g_p1
    o_ref[...] = ((y * _sigmoid(y)) * up).astype(o_ref.dtype)


def _ffn_up(hbf, w_up, layer, conv_w, conv_b, s_len):
    t, d = hbf.shape
    d_ff = w_up.shape[2] // 2
    tm = min(FFN_TM, s_len)
    tn = FFN_TN
    assert d_ff % tn == 0
    nj = d_ff // tn
    n_tiles = t // tm
    tiles_per_seq = s_len // tm
    h3 = hbf.reshape(n_tiles, tm, d)
    edge_x = jnp.stack([h3[:, 0, :], h3[:, tm - 1, :]], axis=1).reshape(2 * n_tiles, d)
    g_edge = _matmul(edge_x, w_up, layer, 2 * n_tiles, tn, F32, "ffn_edge", n_cols=d_ff)
    g_edge = g_edge.reshape(n_tiles, 2, d_ff)
    vm = 2 * (tm * d * 2 + 2 * d * tn * 4 + tm * tn * 2) + 2 * d * tn * 2 + 8 * tm * tn * 4 + (4 << 20)
    w_scr = pltpu.VMEM((d, tn), BF16)
    return pl.pallas_call(
        functools.partial(_ffn_up_kernel, tiles_per_seq=tiles_per_seq),
        grid=(nj, n_tiles),
        in_specs=[pl.BlockSpec((tm, d), lambda j, i: (i, 0)),
                  pl.BlockSpec((None, d, tn), lambda j, i: (layer, 0, j)),
                  pl.BlockSpec((None, d, tn), lambda j, i: (layer, 0, nj + j)),
                  pl.BlockSpec((1, 2, tn), lambda j, i: (jnp.maximum(i - 1, 0), 0, j)),
                  pl.BlockSpec((1, 2, tn), lambda j, i: (jnp.minimum(i + 1, n_tiles - 1), 0, j)),
                  pl.BlockSpec((3, tn), lambda j, i: (0, j)),
                  pl.BlockSpec((1, tn), lambda j, i: (0, j))],
        out_specs=pl.BlockSpec((tm, tn), lambda j, i: (i, j)),
        out_shape=jax.ShapeDtypeStruct((t, d_ff), BF16),
        scratch_shapes=[w_scr, w_scr],
        compiler_params=_params(("parallel", "arbitrary"), vm),
        name="ffn_up",
    )(hbf, w_up, w_up, g_edge, g_edge, conv_w, conv_b.reshape(1, d_ff))


def _rope_tables(s_len):
    rows = s_len // GRID_W
    g_r, g_c = jnp.meshgrid(jnp.arange(rows), jnp.arange(GRID_W), indexing='ij')
    row = g_r.reshape(s_len).astype(F32)
    colp = g_c.reshape(s_len).astype(F32)
    inv_freq = ROPE_THETA ** (-jnp.arange(0, ROPE_HALF, 2, dtype=F32) / ROPE_HALF)
    ang_r = row[:, None] * inv_freq[None, :]
    ang_c = colp[:, None] * inv_freq[None, :]
    ang = jnp.concatenate([ang_r, ang_r, ang_c, ang_c], axis=-1)
    cos, sin = jnp.cos(ang), jnp.sin(ang)
    first_quarter = (jnp.arange(HEAD_DIM) & (ROPE_HALF // 2)) == 0
    sin_a = jnp.where(first_quarter, -sin, 0.0)
    sin_b = jnp.where(first_quarter, 0.0, sin)
    return cos, sin_a, sin_b


def kernel(x, emb_ln_w, emb_ln_b, w_in, hgrn_lb_logits, hgrn_norm_w, rglru_conv_w, rglru_conv_b,
           rglru_wa, rglru_ba, rglru_wx, rglru_bx, rglru_lambda, attn_q_norm_w, attn_k_norm_w,
           w_out, ln1_w, ln1_b, ffn_w_up, ffn_conv_w, ffn_conv_b, ffn_w_down, ln2_w, ln2_b):
    bsz, s_len, d_model = x.shape
    depth = w_in.shape[0]
    t = bsz * s_len
    a_w = hgrn_norm_w.shape[1]
    b_w = rglru_conv_w.shape[2]
    in_cols = w_in.shape[2]
    a_cols = 5 * a_w
    b_cols = 2 * b_w
    c_w = w_out.shape[1] - a_w - b_w
    kv_w = (in_cols - a_cols - b_cols - c_w) // 2
    alpha = (2.0 * depth) ** 0.25

    cos, sin_a, sin_b = _rope_tables(s_len)
    lb_cs = jnp.cumsum(jax.nn.softmax(hgrn_lb_logits.astype(F32), axis=0), axis=0)
    lower_bounds = lb_cs - lb_cs[0:1]

    w_in_bf = w_in.astype(BF16)
    w_out_bf = w_out.astype(BF16)
    w_down_bf = ffn_w_down.astype(BF16)

    h, hbf = _layernorm(x.reshape(t, d_model), emb_ln_w, emb_ln_b)
    for l in range(depth):
        u2 = _matmul(hbf, w_in_bf, l, MM_TM, MM_TN, F32, "in_proj")
        u3 = u2.reshape(bsz, s_len, in_cols)

        ya = _hgrn2(u3, lower_bounds[l], hgrn_norm_w[l], a_w)

        yb = None
        for d in range(2):
            w_dir = jnp.concatenate([rglru_wa[l, d], rglru_wx[l, d]], axis=-1).astype(BF16)
            bias_dir = jnp.stack([rglru_ba[l, d], rglru_bx[l, d]])
            yb = _rglru_dir(u3, a_cols, rglru_conv_w[l], rglru_conv_b[l], w_dir, bias_dir,
                            rglru_lambda[l, d], b_w, hf=yb)
        yb = yb.reshape(t, b_w)

        qr, kr, vt = _attn_prep(u2, a_cols + b_cols, c_w, kv_w, s_len, cos, sin_a, sin_b,
                                attn_q_norm_w[l], attn_k_norm_w[l])
        yc = _attention(qr, kr, vt, bsz, s_len)

        pre = _outproj(ya, yb, yc, w_out_bf, l, h, alpha)
        h, hbf = _layernorm(pre, ln1_w[l], ln1_b[l])

        act = _ffn_up(hbf, ffn_w_up, l, ffn_conv_w[l], ffn_conv_b[l], s_len)
        pre = _matmul_residual(act, w_down_bf, l, h, alpha, DOWN_TM, DOWN_TN, "ffn_down")
        h, hbf = _layernorm(pre, ln2_w[l], ln2_b[l])
    return h.reshape(bsz, s_len, d_model)
```

```python
import functools
import math

import jax
import jax.numpy as jnp
from jax import lax
from jax.experimental import pallas as pl
from jax.experimental.pallas import tpu as pltpu

F32 = jnp.float32
BF16 = jnp.bfloat16

HEAD_DIM = 128
RG_C = 8.0
C_GROUP = 4
ROPE_THETA = 10000.0
ROPE_HALF = HEAD_DIM // 2
GRID_W = 64
LN_EPS = 1e-5
RMS_EPS = 1e-6
ATTN_SCALE = HEAD_DIM ** -0.5
LOG2_E = math.log2(math.e)

V7X_VMEM_BYTES = 64 * 1024 * 1024
SUBLANES = 8
BF16_ROWS = 16

HG_CHUNK = 256
HG_HEADS_PER_STEP = 4
LN_ROWS = 256
MM_TM = 1024
MM_TN = 1024
OUT_TM = 1024
FFN_TM = 1024
FFN_TN = 256
DOWN_TM = 512
DOWN_TN = 512
ATTN_TQ = 2048
ATTN_KB = 1024
PREP_ROWS = 512
RG_ROWS = 512


def _vmem_limit(nbytes):
    return int(min(V7X_VMEM_BYTES - 4 * 1024 * 1024, max(nbytes, 16 * 1024 * 1024)))


def _params(sem, vmem_bytes):
    return pltpu.CompilerParams(dimension_semantics=sem, vmem_limit_bytes=_vmem_limit(vmem_bytes))


def _dot(a, b):
    return jnp.dot(a, b, preferred_element_type=F32)


def _dot_nt(a, b):
    return lax.dot_general(a, b, (((1,), (1,)), ((), ())), preferred_element_type=F32)


def _dot_tn(a, b):
    return lax.dot_general(a, b, (((0,), (0,)), ((), ())), preferred_element_type=F32)


def _sigmoid(x):
    return 1.0 / (1.0 + jnp.exp(-x))


def _ln_kernel(x_ref, w_ref, b_ref, o_ref, obf_ref):
    x = x_ref[...]
    mu = jnp.mean(x, axis=-1, keepdims=True)
    xc = x - mu
    var = jnp.mean(xc * xc, axis=-1, keepdims=True)
    y = xc * lax.rsqrt(var + LN_EPS) * w_ref[...] + b_ref[...]
    o_ref[...] = y
    obf_ref[...] = y.astype(BF16)


def _layernorm(x, w, b):
    t, d = x.shape
    tr = min(LN_ROWS, t)
    return pl.pallas_call(
        _ln_kernel,
        grid=(t // tr,),
        in_specs=[pl.BlockSpec((tr, d), lambda i: (i, 0)),
                  pl.BlockSpec((1, d), lambda i: (0, 0)),
                  pl.BlockSpec((1, d), lambda i: (0, 0))],
        out_specs=[pl.BlockSpec((tr, d), lambda i: (i, 0)),
                   pl.BlockSpec((tr, d), lambda i: (i, 0))],
        out_shape=[jax.ShapeDtypeStruct((t, d), F32), jax.ShapeDtypeStruct((t, d), BF16)],
        compiler_params=_params(("parallel",), 2 * tr * d * 10 + (8 << 20)),
        name="layernorm",
    )(x, w.reshape(1, d), b.reshape(1, d))


def _mm_kernel(x_ref, w_ref, o_ref):
    o_ref[...] = _dot(x_ref[...], w_ref[...]).astype(o_ref.dtype)


def _matmul(x, w, layer, tm, tn, out_dtype, name):
    m, k = x.shape
    n = w.shape[2]
    tm, tn = min(tm, m), min(tn, n)
    vm = 2 * (tm * k * 2 + k * tn * 2 + tm * tn * 4) + tm * tn * 4 + (4 << 20)
    return pl.pallas_call(
        _mm_kernel,
        grid=(m // tm, n // tn),
        in_specs=[pl.BlockSpec((tm, k), lambda i, j: (i, 0)),
                  pl.BlockSpec((None, k, tn), lambda i, j: (layer, 0, j))],
        out_specs=pl.BlockSpec((tm, tn), lambda i, j: (i, j)),
        out_shape=jax.ShapeDtypeStruct((m, n), out_dtype),
        compiler_params=_params(("parallel", "arbitrary"), vm),
        name=name,
    )(x, w)


def _mm_res_kernel(x_ref, w_ref, r_ref, o_ref, *, alpha):
    o_ref[...] = alpha * r_ref[...] + _dot(x_ref[...], w_ref[...])


def _matmul_residual(x, w, layer, res, alpha, tm, tn, name):
    m, k = x.shape
    n = w.shape[2]
    tm, tn = min(tm, m), min(tn, n)
    vm = 2 * (tm * k * 2 + k * tn * 2 + 2 * tm * tn * 4) + tm * tn * 4 + (4 << 20)
    return pl.pallas_call(
        functools.partial(_mm_res_kernel, alpha=alpha),
        grid=(m // tm, n // tn),
        in_specs=[pl.BlockSpec((tm, k), lambda i, j: (i, 0)),
                  pl.BlockSpec((None, k, tn), lambda i, j: (layer, 0, j)),
                  pl.BlockSpec((tm, tn), lambda i, j: (i, j))],
        out_specs=pl.BlockSpec((tm, tn), lambda i, j: (i, j)),
        out_shape=jax.ShapeDtypeStruct((m, n), F32),
        compiler_params=_params(("parallel", "arbitrary"), vm),
        name=name,
    )(x, w, res)


def _hgrn_chunk(q, z, v, lb, st_ref, rev, row, pair_code):
    c_len = q.shape[0]
    n_lvl = c_len.bit_length() - 1
    t = jnp.exp(-jnp.abs(z))
    r = 1.0 / (1.0 + t)
    tr = t * r
    pos = z >= 0.0
    f = lb + (1.0 - lb) * jnp.where(pos, r, tr)
    kk = (1.0 - lb) * jnp.where(pos, tr, r)

    vb = v.astype(BF16)
    scores = _dot_nt(q.astype(BF16), kk.astype(BF16))
    ep = f
    ex = jnp.ones_like(f)
    et = f
    for lvl in range(n_lvl):
        m = 1 << lvl
        later = ((row & m) == 0) if rev else ((row & m) != 0)
        x = jnp.where(later, q * ep, kk * ex).astype(BF16)
        scores = jnp.where(pair_code >= m, _dot_nt(x, x), scores)
        from_earlier, from_later = (c_len - m, m) if rev else (m, c_len - m)
        sib = jnp.where(later, pltpu.roll(et, from_earlier, 0), pltpu.roll(et, from_later, 0))
        ep = jnp.where(later, ep * sib, ep)
        ex = jnp.where(later, ex, ex * sib)
        et = et * sib
    scores = jnp.where(pair_code < 0, 0.0, scores)

    st = st_ref[...]
    o = _dot_nt((q * ep).astype(BF16), st.astype(BF16)) + _dot(scores.astype(BF16), vb)
    st_ref[...] = et[0:1, :] * st + _dot_tn(vb, (kk * ex).astype(BF16))
    return o


def _hgrn_kernel(q_ref, zf_ref, zb_ref, v_ref, g_ref, lb_ref, nw_ref, o_ref,
                 of_scr, ob_scr, stf_scr, stb_scr, *, chunk):
    s_len = q_ref.shape[0]
    n_chunks = s_len // chunk
    row = lax.broadcasted_iota(jnp.int32, (chunk, HEAD_DIM), 0)
    ri = lax.broadcasted_iota(jnp.int32, (chunk, chunk), 0)
    ci = lax.broadcasted_iota(jnp.int32, (chunk, chunk), 1)
    code_f = jnp.where(ri >= ci, ri ^ ci, -1)
    code_b = jnp.where(ci >= ri, ri ^ ci, -1)
    n_heads = q_ref.shape[1] // HEAD_DIM
    stf_scr[...] = jnp.zeros_like(stf_scr)
    stb_scr[...] = jnp.zeros_like(stb_scr)

    def body(i, carry):
        rf = pl.multiple_of(i * chunk, chunk)
        rb = pl.multiple_of((n_chunks - 1 - i) * chunk, chunk)
        for hd in range(n_heads):
            sl = slice(hd * HEAD_DIM, (hd + 1) * HEAD_DIM)
            of_scr[pl.ds(rf, chunk), sl] = _hgrn_chunk(
                q_ref[pl.ds(rf, chunk), sl], zf_ref[pl.ds(rf, chunk), sl], v_ref[pl.ds(rf, chunk), sl],
                lb_ref[0:1, sl], stf_scr.at[hd], False, row, code_f)
            ob_scr[pl.ds(rb, chunk), sl] = _hgrn_chunk(
                q_ref[pl.ds(rb, chunk), sl], zb_ref[pl.ds(rb, chunk), sl], v_ref[pl.ds(rb, chunk), sl],
                lb_ref[1:2, sl], stb_scr.at[hd], True, row, code_b)
        return carry

    lax.fori_loop(0, n_chunks, body, 0)

    def epilogue(i, carry):
        r = pl.multiple_of(i * chunk, chunk)
        for hd in range(n_heads):
            sl = slice(hd * HEAD_DIM, (hd + 1) * HEAD_DIM)
            o = of_scr[pl.ds(r, chunk), sl] + ob_scr[pl.ds(r, chunk), sl]
            y = o * lax.rsqrt(jnp.mean(o * o, axis=-1, keepdims=True) + RMS_EPS) * nw_ref[:, sl]
            g = g_ref[pl.ds(r, chunk), sl]
            o_ref[pl.ds(r, chunk), sl] = (y * (g * _sigmoid(g))).astype(o_ref.dtype)
        return carry

    lax.fori_loop(0, n_chunks, epilogue, 0)


def _hgrn2(u3, lb, norm_w, a_w):
    bsz, s_len, _ = u3.shape
    heads = a_w // HEAD_DIM
    chunk = min(HG_CHUNK, s_len)
    hps = HG_HEADS_PER_STEP
    wid = hps * HEAD_DIM
    groups = heads // hps

    def col(k):
        return pl.BlockSpec((None, s_len, wid), lambda b, h: (b, 0, k * groups + h))

    out = pl.pallas_call(
        functools.partial(_hgrn_kernel, chunk=chunk),
        grid=(bsz, groups),
        in_specs=[col(0), col(1), col(2), col(3), col(4),
                  pl.BlockSpec((2, wid), lambda b, h: (0, h)),
                  pl.BlockSpec((1, wid), lambda b, h: (0, h))],
        out_specs=pl.BlockSpec((None, s_len, wid), lambda b, h: (b, 0, h)),
        out_shape=jax.ShapeDtypeStruct((bsz, s_len, a_w), BF16),
        scratch_shapes=[pltpu.VMEM((s_len, wid), F32), pltpu.VMEM((s_len, wid), F32),
                        pltpu.VMEM((hps, HEAD_DIM, HEAD_DIM), F32), pltpu.VMEM((hps, HEAD_DIM, HEAD_DIM), F32)],
        compiler_params=_params(("parallel", "arbitrary"), 14 * s_len * wid * 4 + (16 << 20)),
        name="hgrn2",
    )(u3, u3, u3, u3, u3, lb, norm_w.reshape(1, a_w))
    return out.reshape(bsz * s_len, a_w)


def _gelu_tanh(x):
    return 0.5 * x * (1.0 + jnp.tanh(math.sqrt(2.0 / math.pi) * (x + 0.044715 * (x * x * x))))


def _rglru_kernel(x_ref, xp_ref, xn_ref, cw_ref, cb_ref, w_ref, bias_ref, lam_ref, *rest, rev):
    if rev:
        gate_ref, hf_ref, o_ref, carry_scr = rest
    else:
        o_ref, carry_scr = rest
    ts, width = x_ref.shape
    n_grp = ts // SUBLANES
    step = pl.program_id(1)
    tile = pl.num_programs(1) - 1 - step if rev else step

    @pl.when(step == 0)
    def _():
        carry_scr[...] = jnp.zeros_like(carry_scr)

    has_prev = (tile > 0).astype(F32)
    has_next = (tile < pl.num_programs(1) - 1).astype(F32)
    x = x_ref[...]
    row = lax.broadcasted_iota(jnp.int32, (ts, width), 0)
    p6 = xp_ref[SUBLANES - 2:SUBLANES - 1, :] * has_prev
    p7 = xp_ref[SUBLANES - 1:SUBLANES, :] * has_prev
    n0 = xn_ref[0:1, :] * has_next
    xm1 = jnp.where(row == 0, p7, pltpu.roll(x, 1, 0))
    xm2 = jnp.where(row == 0, p6, jnp.where(row == 1, p7, pltpu.roll(x, 2, 0)))
    xp1 = jnp.where(row == ts - 1, n0, pltpu.roll(x, ts - 1, 0))
    xc = (cb_ref[...] + cw_ref[0:1, :] * xm2 + cw_ref[1:2, :] * xm1
          + cw_ref[2:3, :] * x + cw_ref[3:4, :] * xp1)
    lam = lam_ref[...]
    sp = jnp.maximum(-lam, 0.0) + jnp.log1p(jnp.exp(-jnp.abs(lam)))
    sub = lax.broadcasted_iota(jnp.int32, (n_grp, SUBLANES, HEAD_DIM), 1)
    for n in range(width // HEAD_DIM):
        sl = slice(n * HEAD_DIM, (n + 1) * HEAD_DIM)
        xn_blk = xc[:, sl]
        zz = _dot(xn_blk.astype(BF16), w_ref[n])
        r = _sigmoid(zz[:, :HEAD_DIM] + bias_ref[0:1, sl])
        ig = _sigmoid(zz[:, HEAD_DIM:] + bias_ref[1:2, sl])
        log_a = (-RG_C) * r * sp[:, sl]
        a = jnp.exp(log_a)
        b = jnp.sqrt(-jnp.tanh(log_a) * (a * a + 1.0)) * (ig * xn_blk)
        a = a.reshape(n_grp, SUBLANES, HEAD_DIM)
        b = b.reshape(n_grp, SUBLANES, HEAD_DIM)
        for d in (1, 2, 4):
            if rev:
                a_s, b_s, ok = pltpu.roll(a, SUBLANES - d, 1), pltpu.roll(b, SUBLANES - d, 1), sub < SUBLANES - d
            else:
                a_s, b_s, ok = pltpu.roll(a, d, 1), pltpu.roll(b, d, 1), sub >= d
            b = jnp.where(ok, a * b_s + b, b)
            a = jnp.where(ok, a * a_s, a)
        h_in = jnp.broadcast_to(carry_scr[:, sl], (SUBLANES, HEAD_DIM))
        last = 0 if rev else SUBLANES - 1
        for g in (range(n_grp - 1, -1, -1) if rev else range(n_grp)):
            rows = slice(g * SUBLANES, (g + 1) * SUBLANES)
            h = a[g] * h_in + b[g]
            h_in = jnp.broadcast_to(h[last:last + 1, :], (SUBLANES, HEAD_DIM))
            if rev:
                o_ref[rows, sl] = (_gelu_tanh(gate_ref[rows, sl]) * (hf_ref[rows, sl] + h)).astype(o_ref.dtype)
            else:
                o_ref[rows, sl] = h
        carry_scr[:, sl] = h_in[0:1, :]


def _rglru_dir(u3, col0, conv_w, conv_b, w_dir, bias_dir, lam_dir, b_w, hf=None):
    bsz, s_len, _ = u3.shape
    rev = hf is not None
    ts = min(RG_ROWS, s_len)
    nt = s_len // ts
    cb = col0 // b_w
    rows8 = ts // SUBLANES

    def tile(i):
        return nt - 1 - i if rev else i

    in_specs = [pl.BlockSpec((None, ts, b_w), lambda b, i: (b, tile(i), cb)),
                pl.BlockSpec((None, SUBLANES, b_w), lambda b, i: (b, jnp.maximum(tile(i) * rows8 - 1, 0), cb)),
                pl.BlockSpec((None, SUBLANES, b_w),
                             lambda b, i: (b, jnp.minimum((tile(i) + 1) * rows8, s_len // SUBLANES - 1), cb)),
                pl.BlockSpec((4, b_w), lambda b, i: (0, 0)),
                pl.BlockSpec((1, b_w), lambda b, i: (0, 0)),
                pl.BlockSpec(w_dir.shape, lambda b, i: (0, 0, 0)),
                pl.BlockSpec((2, b_w), lambda b, i: (0, 0)),
                pl.BlockSpec((1, b_w), lambda b, i: (0, 0))]
    args = [u3, u3, u3, conv_w, conv_b.reshape(1, b_w), w_dir, bias_dir, lam_dir.reshape(1, b_w)]
    if rev:
        in_specs += [pl.BlockSpec((None, ts, b_w), lambda b, i: (b, tile(i), cb + 1)),
                     pl.BlockSpec((None, ts, b_w), lambda b, i: (b, tile(i), 0))]
        args += [u3, hf]
    return pl.pallas_call(
        functools.partial(_rglru_kernel, rev=rev),
        grid=(bsz, nt),
        in_specs=in_specs,
        out_specs=pl.BlockSpec((None, ts, b_w), lambda b, i: (b, tile(i), 0)),
        out_shape=jax.ShapeDtypeStruct((bsz, s_len, b_w), BF16 if rev else F32),
        scratch_shapes=[pltpu.VMEM((1, b_w), F32)],
        compiler_params=_params(("parallel", "arbitrary"), 2 * 4 * ts * b_w * 4 + (24 << 20)),
        name="rglru_bwd" if rev else "rglru_fwd",
    )(*args)


def _attn_prep_kernel(q0_ref, q1_ref, kv_ref, cos_ref, sa_ref, sb_ref, qn_ref, kn_ref,
                      qo_ref, ko_ref, vo_ref):
    cos = cos_ref[...]
    sin_a = sa_ref[...]
    sin_b = sb_ref[...]

    def norm_rope(t, w):
        y = t * lax.rsqrt(jnp.mean(t * t, axis=-1, keepdims=True) + RMS_EPS) * w
        return (y * cos + pltpu.roll(y, HEAD_DIM - ROPE_HALF // 2, 1) * sin_a
                + pltpu.roll(y, ROPE_HALF // 2, 1) * sin_b)

    half_heads = q0_ref.shape[1] // HEAD_DIM
    kv_heads = ko_ref.shape[1] // HEAD_DIM
    qn = qn_ref[...]
    kn = kn_ref[...]
    for hd in range(half_heads):
        sl = slice(hd * HEAD_DIM, (hd + 1) * HEAD_DIM)
        so = slice((half_heads + hd) * HEAD_DIM, (half_heads + hd + 1) * HEAD_DIM)
        qo_ref[:, sl] = norm_rope(q0_ref[:, sl], qn).astype(BF16)
        qo_ref[:, so] = norm_rope(q1_ref[:, sl], qn).astype(BF16)
    for hd in range(kv_heads):
        sl = slice(hd * HEAD_DIM, (hd + 1) * HEAD_DIM)
        ko_ref[:, sl] = norm_rope(kv_ref[:, sl], kn).astype(BF16)
    vo_ref[...] = kv_ref[:, kv_heads * HEAD_DIM:].T.astype(BF16)


def _attn_prep(u2, col0, c_w, kv_w, s_len, cos, sin_a, sin_b, qn_w, kn_w):
    t = u2.shape[0]
    tr = min(PREP_ROWS, s_len)
    half = c_w // 2
    assert col0 % half == 0 and 2 * kv_w == half
    cb = col0 // half
    npos = s_len // tr
    tab = pl.BlockSpec((tr, HEAD_DIM), lambda i: (i % npos, 0))
    vec = pl.BlockSpec((1, HEAD_DIM), lambda i: (0, 0))
    return pl.pallas_call(
        _attn_prep_kernel,
        grid=(t // tr,),
        in_specs=[pl.BlockSpec((tr, half), lambda i: (i, cb)),
                  pl.BlockSpec((tr, half), lambda i: (i, cb + 1)),
                  pl.BlockSpec((tr, half), lambda i: (i, cb + 2)),
                  tab, tab, tab, vec, vec],
        out_specs=[pl.BlockSpec((tr, c_w), lambda i: (i, 0)),
                   pl.BlockSpec((tr, kv_w), lambda i: (i, 0)),
                   pl.BlockSpec((kv_w, tr), lambda i: (0, i))],
        out_shape=[jax.ShapeDtypeStruct((t, c_w), BF16),
                   jax.ShapeDtypeStruct((t, kv_w), BF16),
                   jax.ShapeDtypeStruct((kv_w, t), BF16)],
        compiler_params=_params(("parallel",), 2 * tr * (3 * half * 4 + (c_w + 2 * kv_w) * 2) + (16 << 20)),
        name="attn_prep",
    )(u2, u2, u2, cos, sin_a, sin_b, qn_w.reshape(1, HEAD_DIM), kn_w.reshape(1, HEAD_DIM))


def _attn_kernel(q_ref, k_ref, vt_ref, o_ref, sc_scr):
    k = k_ref[...]
    vt = vt_ref[...]
    vt_aug = jnp.concatenate([vt, jnp.ones((BF16_ROWS, vt.shape[1]), BF16)], axis=0)
    s_len = k.shape[0]
    kc = min(ATTN_KB, s_len)
    n_kc = s_len // kc

    def scores(g, c, run_max):
        rows = slice(c * kc, (c + 1) * kc)
        sc = _dot_nt(k[rows, :], q_ref[:, g * HEAD_DIM:(g + 1) * HEAD_DIM])
        sc_scr[g % 2, rows, :] = sc
        cmax = jnp.max(sc.reshape(kc // SUBLANES, SUBLANES, sc.shape[1]), axis=0)
        return cmax if run_max is None else jnp.maximum(run_max, cmax)

    next_max = None
    for c in range(n_kc):
        next_max = scores(0, c, next_max)
    for g in range(C_GROUP):
        mx = jnp.max(next_max, axis=0, keepdims=True)
        next_max = None
        acc = None
        for c in range(n_kc):
            rows = slice(c * kc, (c + 1) * kc)
            p = jnp.exp2((sc_scr[g % 2, rows, :] - mx) * (ATTN_SCALE * LOG2_E)).astype(BF16)
            if g + 1 < C_GROUP:
                next_max = scores(g + 1, c, next_max)
            part = _dot(vt_aug[:, rows], p)
            acc = part if acc is None else acc + part
        ot = acc[:HEAD_DIM, :] / acc[HEAD_DIM:HEAD_DIM + 1, :]
        o_ref[:, g * HEAD_DIM:(g + 1) * HEAD_DIM] = ot.T.astype(o_ref.dtype)


def _attention(qr, kr, vt, bsz, s_len):
    t, c_w = qr.shape
    kv_heads = kr.shape[1] // HEAD_DIM
    gw = C_GROUP * HEAD_DIM
    tq = min(ATTN_TQ, s_len)
    nq = s_len // tq
    return pl.pallas_call(
        _attn_kernel,
        grid=(bsz, kv_heads, nq),
        in_specs=[pl.BlockSpec((tq, gw), lambda b, h, i: (b * nq + i, h)),
                  pl.BlockSpec((s_len, HEAD_DIM), lambda b, h, i: (b, h)),
                  pl.BlockSpec((HEAD_DIM, s_len), lambda b, h, i: (h, b))],
        out_specs=pl.BlockSpec((tq, gw), lambda b, h, i: (b * nq + i, h)),
        out_shape=jax.ShapeDtypeStruct((t, c_w), BF16),
        scratch_shapes=[pltpu.VMEM((2, s_len, tq), F32)],
        compiler_params=_params(("parallel", "parallel", "arbitrary"),
                                8 * tq * s_len * 4 + 4 * s_len * HEAD_DIM * 2 + (8 << 20)),
        name="attention",
    )(qr, kr, vt)


def _outproj_kernel(ya_ref, yb_ref, yc_ref, wa_ref, wb_ref, wc_ref, res_ref, o_ref, *, alpha):
    acc = _dot(ya_ref[...], wa_ref[...])
    acc = acc + _dot(yb_ref[...], wb_ref[...])
    acc = acc + _dot(yc_ref[...], wc_ref[...])
    o_ref[...] = alpha * res_ref[...] + acc


def _outproj(ya, yb, yc, w_out, layer, res, alpha):
    t, a_w = ya.shape
    b_w = yb.shape[1]
    c_w = yc.shape[1]
    n = w_out.shape[2]
    assert a_w == b_w and (a_w + b_w) % c_w == 0
    tm = min(OUT_TM, t)
    tn = min(MM_TN, n)
    c_blk = (a_w + b_w) // c_w
    vm = 2 * (tm * (a_w + b_w + c_w) * 2 + (a_w + b_w + c_w) * tn * 2 + 2 * tm * tn * 4) + tm * tn * 4 + (4 << 20)
    return pl.pallas_call(
        functools.partial(_outproj_kernel, alpha=alpha),
        grid=(t // tm, n // tn),
        in_specs=[pl.BlockSpec((tm, a_w), lambda i, j: (i, 0)),
                  pl.BlockSpec((tm, b_w), lambda i, j: (i, 0)),
                  pl.BlockSpec((tm, c_w), lambda i, j: (i, 0)),
                  pl.BlockSpec((None, a_w, tn), lambda i, j: (layer, 0, j)),
                  pl.BlockSpec((None, b_w, tn), lambda i, j: (layer, 1, j)),
                  pl.BlockSpec((None, c_w, tn), lambda i, j: (layer, c_blk, j)),
                  pl.BlockSpec((tm, tn), lambda i, j: (i, j))],
        out_specs=pl.BlockSpec((tm, tn), lambda i, j: (i, j)),
        out_shape=jax.ShapeDtypeStruct((t, n), F32),
        compiler_params=_params(("parallel", "arbitrary"), vm),
        name="outproj",
    )(ya, yb, yc, w_out, w_out, w_out, res)


def _ffn_up_kernel(x_ref, ex_ref, wg_ref, wu_ref, cw_ref, cb_ref, o_ref, wg_scr, wu_scr, edge_scr,
                   *, tiles_per_seq):
    tm, tn = o_ref.shape
    i = pl.program_id(1)

    @pl.when(i == 0)
    def _():
        wg_scr[...] = wg_ref[...].astype(BF16)
        wu_scr[...] = wu_ref[...].astype(BF16)
        edge_scr[...] = _dot(ex_ref[...], wg_scr[...])

    x = x_ref[...]
    g = _dot(x, wg_scr[...])
    up = _dot(x, wu_scr[...])
    pos = i % tiles_per_seq
    has_prev = (pos > 0).astype(F32)
    has_next = (pos < tiles_per_seq - 1).astype(F32)
    last_edge = edge_scr.shape[0] - 1
    g_prev = edge_scr[pl.ds(jnp.maximum(2 * i - 1, 0), 1), :] * has_prev
    g_next = edge_scr[pl.ds(jnp.minimum(2 * i + 2, last_edge), 1), :] * has_next
    row = lax.broadcasted_iota(jnp.int32, (tm, tn), 0)
    g_m1 = jnp.where(row == 0, g_prev, pltpu.roll(g, 1, 0))
    g_p1 = jnp.where(row == tm - 1, g_next, pltpu.roll(g, tm - 1, 0))
    y = cb_ref[...] + cw_ref[0:1, :] * g_m1 + cw_ref[1:2, :] * g + cw_ref[2:3, :] * g_p1
    o_ref[...] = ((y * _sigmoid(y)) * up).astype(o_ref.dtype)


def _ffn_up(hbf, w_up, layer, conv_w, conv_b, s_len):
    t, d = hbf.shape
    d_ff = w_up.shape[2] // 2
    tm = min(FFN_TM, s_len)
    tn = FFN_TN
    assert d_ff % tn == 0
    nj = d_ff // tn
    n_tiles = t // tm
    tiles_per_seq = s_len // tm
    h3 = hbf.reshape(n_tiles, tm, d)
    edge_x = jnp.stack([h3[:, 0, :], h3[:, tm - 1, :]], axis=1).reshape(2 * n_tiles, d)
    vm = 2 * (tm * d * 2 + 2 * d * tn * 4 + tm * tn * 2) + 2 * d * tn * 2 + 8 * tm * tn * 4 + (4 << 20)
    w_scr = pltpu.VMEM((d, tn), BF16)
    return pl.pallas_call(
        functools.partial(_ffn_up_kernel, tiles_per_seq=tiles_per_seq),
        grid=(nj, n_tiles),
        in_specs=[pl.BlockSpec((tm, d), lambda j, i: (i, 0)),
                  pl.BlockSpec((2 * n_tiles, d), lambda j, i: (0, 0)),
                  pl.BlockSpec((None, d, tn), lambda j, i: (layer, 0, j)),
                  pl.BlockSpec((None, d, tn), lambda j, i: (layer, 0, nj + j)),
                  pl.BlockSpec((3, tn), lambda j, i: (0, j)),
                  pl.BlockSpec((1, tn), lambda j, i: (0, j))],
        out_specs=pl.BlockSpec((tm, tn), lambda j, i: (i, j)),
        out_shape=jax.ShapeDtypeStruct((t, d_ff), BF16),
        scratch_shapes=[w_scr, w_scr, pltpu.VMEM((2 * n_tiles, tn), F32)],
        compiler_params=_params(("parallel", "arbitrary"), vm),
        name="ffn_up",
    )(hbf, edge_x, w_up, w_up, conv_w, conv_b.reshape(1, d_ff))


def _rope_tables(s_len):
    rows = s_len // GRID_W
    g_r, g_c = jnp.meshgrid(jnp.arange(rows), jnp.arange(GRID_W), indexing='ij')
    row = g_r.reshape(s_len).astype(F32)
    colp = g_c.reshape(s_len).astype(F32)
    inv_freq = ROPE_THETA ** (-jnp.arange(0, ROPE_HALF, 2, dtype=F32) / ROPE_HALF)
    ang_r = row[:, None] * inv_freq[None, :]
    ang_c = colp[:, None] * inv_freq[None, :]
    ang = jnp.concatenate([ang_r, ang_r, ang_c, ang_c], axis=-1)
    cos, sin = jnp.cos(ang), jnp.sin(ang)
    first_quarter = (jnp.arange(HEAD_DIM) & (ROPE_HALF // 2)) == 0
    sin_a = jnp.where(first_quarter, -sin, 0.0)
    sin_b = jnp.where(first_quarter, 0.0, sin)
    return cos, sin_a, sin_b


def kernel(x, emb_ln_w, emb_ln_b, w_in, hgrn_lb_logits, hgrn_norm_w, rglru_conv_w, rglru_conv_b,
           rglru_wa, rglru_ba, rglru_wx, rglru_bx, rglru_lambda, attn_q_norm_w, attn_k_norm_w,
           w_out, ln1_w, ln1_b, ffn_w_up, ffn_conv_w, ffn_conv_b, ffn_w_down, ln2_w, ln2_b):
    bsz, s_len, d_model = x.shape
    depth = w_in.shape[0]
    t = bsz * s_len
    a_w = hgrn_norm_w.shape[1]
    b_w = rglru_conv_w.shape[2]
    in_cols = w_in.shape[2]
    a_cols = 5 * a_w
    b_cols = 2 * b_w
    c_w = w_out.shape[1] - a_w - b_w
    kv_w = (in_cols - a_cols - b_cols - c_w) // 2
    alpha = (2.0 * depth) ** 0.25

    cos, sin_a, sin_b = _rope_tables(s_len)
    lb_cs = jnp.cumsum(jax.nn.softmax(hgrn_lb_logits.astype(F32), axis=0), axis=0)
    lower_bounds = lb_cs - lb_cs[0:1]

    w_in_bf = w_in.astype(BF16)
    w_out_bf = w_out.astype(BF16)
    w_down_bf = ffn_w_down.astype(BF16)

    h, hbf = _layernorm(x.reshape(t, d_model), emb_ln_w, emb_ln_b)
    for l in range(depth):
        u2 = _matmul(hbf, w_in_bf, l, MM_TM, MM_TN, F32, "in_proj")
        u3 = u2.reshape(bsz, s_len, in_cols)

        ya = _hgrn2(u3, lower_bounds[l], hgrn_norm_w[l], a_w)

        yb = None
        for d in range(2):
            w_dir = jnp.concatenate([rglru_wa[l, d], rglru_wx[l, d]], axis=-1).astype(BF16)
            bias_dir = jnp.stack([rglru_ba[l, d], rglru_bx[l, d]])
            yb = _rglru_dir(u3, a_cols, rglru_conv_w[l], rglru_conv_b[l], w_dir, bias_dir,
                            rglru_lambda[l, d], b_w, hf=yb)
        yb = yb.reshape(t, b_w)

        qr, kr, vt = _attn_prep(u2, a_cols + b_cols, c_w, kv_w, s_len, cos, sin_a, sin_b,
                                attn_q_norm_w[l], attn_k_norm_w[l])
        yc = _attention(qr, kr, vt, bsz, s_len)

        pre = _outproj(ya, yb, yc, w_out_bf, l, h, alpha)
        h, hbf = _layernorm(pre, ln1_w[l], ln1_b[l])

        act = _ffn_up(hbf, ffn_w_up, l, ffn_conv_w[l], ffn_conv_b[l], s_len)
        pre = _matmul_residual(act, w_down_bf, l, h, alpha, DOWN_TM, DOWN_TN, "ffn_down")
        h, hbf = _layernorm(pre, ln2_w[l], ln2_b[l])
    return h.reshape(bsz, s_len, d_model)
```

```python
import functools
import math

import jax
import jax.numpy as jnp
from jax import lax
from jax.experimental import pallas as pl
from jax.experimental.pallas import tpu as pltpu

F32 = jnp.float32
BF16 = jnp.bfloat16

HEAD_DIM = 128
RG_C = 8.0
C_GROUP = 4
ROPE_THETA = 10000.0
ROPE_HALF = HEAD_DIM // 2
GRID_W = 64
LN_EPS = 1e-5
RMS_EPS = 1e-6
ATTN_SCALE = HEAD_DIM ** -0.5
LOG2_E = math.log2(math.e)

V7X_VMEM_BYTES = 64 * 1024 * 1024
SUBLANES = 8
BF16_ROWS = 16

HG_CHUNK = 256
HG_HEADS_PER_STEP = 4
LN_ROWS = 256
MM_TM = 1024
MM_TN = 1024
IN_TN = 512
OUT_TM = 1024
FFN_TM = 1024
FFN_TN = 256
DOWN_TM = 512
DOWN_TN = 512
ATTN_TQ = 2048
ATTN_KB = 1024
PREP_ROWS = 512
RG_ROWS = 512


def _vmem_limit(nbytes):
    return int(min(V7X_VMEM_BYTES - 4 * 1024 * 1024, max(nbytes, 16 * 1024 * 1024)))


def _params(sem, vmem_bytes):
    return pltpu.CompilerParams(dimension_semantics=sem, vmem_limit_bytes=_vmem_limit(vmem_bytes))


def _dot(a, b):
    return jnp.dot(a, b, preferred_element_type=F32)


def _dot_nt(a, b):
    return lax.dot_general(a, b, (((1,), (1,)), ((), ())), preferred_element_type=F32)


def _dot_tn(a, b):
    return lax.dot_general(a, b, (((0,), (0,)), ((), ())), preferred_element_type=F32)


def _sigmoid(x):
    return 1.0 / (1.0 + jnp.exp(-x))


def _ln_kernel(x_ref, w_ref, b_ref, o_ref, obf_ref):
    x = x_ref[...]
    mu = jnp.mean(x, axis=-1, keepdims=True)
    xc = x - mu
    var = jnp.mean(xc * xc, axis=-1, keepdims=True)
    y = xc * lax.rsqrt(var + LN_EPS) * w_ref[...] + b_ref[...]
    o_ref[...] = y
    obf_ref[...] = y.astype(BF16)


def _layernorm(x, w, b):
    t, d = x.shape
    tr = min(LN_ROWS, t)
    return pl.pallas_call(
        _ln_kernel,
        grid=(t // tr,),
        in_specs=[pl.BlockSpec((tr, d), lambda i: (i, 0)),
                  pl.BlockSpec((1, d), lambda i: (0, 0)),
                  pl.BlockSpec((1, d), lambda i: (0, 0))],
        out_specs=[pl.BlockSpec((tr, d), lambda i: (i, 0)),
                   pl.BlockSpec((tr, d), lambda i: (i, 0))],
        out_shape=[jax.ShapeDtypeStruct((t, d), F32), jax.ShapeDtypeStruct((t, d), BF16)],
        compiler_params=_params(("parallel",), 2 * tr * d * 10 + (8 << 20)),
        name="layernorm",
    )(x, w.reshape(1, d), b.reshape(1, d))


def _mm_cast_kernel(x_ref, w_ref, o_ref, w_scr):
    @pl.when(pl.program_id(1) == 0)
    def _():
        w_scr[...] = w_ref[...].astype(BF16)

    o_ref[...] = _dot(x_ref[...], w_scr[...])


def _matmul_f32_weights(x, w, layer, tm, tn, name):
    m, k = x.shape
    n = w.shape[2]
    tm, tn = min(tm, m), min(tn, n)
    vm = 2 * (tm * k * 2 + k * tn * 4 + tm * tn * 4) + k * tn * 2 + tm * tn * 4 + (4 << 20)
    return pl.pallas_call(
        _mm_cast_kernel,
        grid=(n // tn, m // tm),
        in_specs=[pl.BlockSpec((tm, k), lambda j, i: (i, 0)),
                  pl.BlockSpec((None, k, tn), lambda j, i: (layer, 0, j))],
        out_specs=pl.BlockSpec((tm, tn), lambda j, i: (i, j)),
        out_shape=jax.ShapeDtypeStruct((m, n), F32),
        scratch_shapes=[pltpu.VMEM((k, tn), BF16)],
        compiler_params=_params(("parallel", "arbitrary"), vm),
        name=name,
    )(x, w)


def _mm_res_kernel(x_ref, w_ref, r_ref, o_ref, *, alpha):
    o_ref[...] = alpha * r_ref[...] + _dot(x_ref[...], w_ref[...])


def _matmul_residual(x, w, layer, res, alpha, tm, tn, name):
    m, k = x.shape
    n = w.shape[2]
    tm, tn = min(tm, m), min(tn, n)
    vm = 2 * (tm * k * 2 + k * tn * 2 + 2 * tm * tn * 4) + tm * tn * 4 + (4 << 20)
    return pl.pallas_call(
        functools.partial(_mm_res_kernel, alpha=alpha),
        grid=(m // tm, n // tn),
        in_specs=[pl.BlockSpec((tm, k), lambda i, j: (i, 0)),
                  pl.BlockSpec((None, k, tn), lambda i, j: (layer, 0, j)),
                  pl.BlockSpec((tm, tn), lambda i, j: (i, j))],
        out_specs=pl.BlockSpec((tm, tn), lambda i, j: (i, j)),
        out_shape=jax.ShapeDtypeStruct((m, n), F32),
        compiler_params=_params(("parallel", "arbitrary"), vm),
        name=name,
    )(x, w, res)


def _hgrn_chunk(q, z, v, lb, st_ref, rev, row, pair_code):
    c_len = q.shape[0]
    n_lvl = c_len.bit_length() - 1
    t = jnp.exp(-jnp.abs(z))
    r = 1.0 / (1.0 + t)
    tr = t * r
    pos = z >= 0.0
    f = lb + (1.0 - lb) * jnp.where(pos, r, tr)
    kk = (1.0 - lb) * jnp.where(pos, tr, r)

    vb = v.astype(BF16)
    scores = _dot_nt(q.astype(BF16), kk.astype(BF16))
    ep = f
    ex = jnp.ones_like(f)
    et = f
    for lvl in range(n_lvl):
        m = 1 << lvl
        later = ((row & m) == 0) if rev else ((row & m) != 0)
        x = jnp.where(later, q * ep, kk * ex).astype(BF16)
        scores = jnp.where(pair_code >= m, _dot_nt(x, x), scores)
        from_earlier, from_later = (c_len - m, m) if rev else (m, c_len - m)
        sib = jnp.where(later, pltpu.roll(et, from_earlier, 0), pltpu.roll(et, from_later, 0))
        ep = jnp.where(later, ep * sib, ep)
        ex = jnp.where(later, ex, ex * sib)
        et = et * sib
    scores = jnp.where(pair_code < 0, 0.0, scores)

    st = st_ref[...]
    o = _dot_nt((q * ep).astype(BF16), st.astype(BF16)) + _dot(scores.astype(BF16), vb)
    st_ref[...] = et[0:1, :] * st + _dot_tn(vb, (kk * ex).astype(BF16))
    return o


def _hgrn_kernel(q_ref, zf_ref, zb_ref, v_ref, g_ref, lb_ref, nw_ref, o_ref,
                 of_scr, ob_scr, stf_scr, stb_scr, *, chunk):
    s_len = q_ref.shape[0]
    n_chunks = s_len // chunk
    row = lax.broadcasted_iota(jnp.int32, (chunk, HEAD_DIM), 0)
    ri = lax.broadcasted_iota(jnp.int32, (chunk, chunk), 0)
    ci = lax.broadcasted_iota(jnp.int32, (chunk, chunk), 1)
    code_f = jnp.where(ri >= ci, ri ^ ci, -1)
    code_b = jnp.where(ci >= ri, ri ^ ci, -1)
    n_heads = q_ref.shape[1] // HEAD_DIM
    stf_scr[...] = jnp.zeros_like(stf_scr)
    stb_scr[...] = jnp.zeros_like(stb_scr)

    def body(i, carry):
        rf = pl.multiple_of(i * chunk, chunk)
        rb = pl.multiple_of((n_chunks - 1 - i) * chunk, chunk)
        for hd in range(n_heads):
            sl = slice(hd * HEAD_DIM, (hd + 1) * HEAD_DIM)
            of_scr[pl.ds(rf, chunk), sl] = _hgrn_chunk(
                q_ref[pl.ds(rf, chunk), sl], zf_ref[pl.ds(rf, chunk), sl], v_ref[pl.ds(rf, chunk), sl],
                lb_ref[0:1, sl], stf_scr.at[hd], False, row, code_f)
            ob_scr[pl.ds(rb, chunk), sl] = _hgrn_chunk(
                q_ref[pl.ds(rb, chunk), sl], zb_ref[pl.ds(rb, chunk), sl], v_ref[pl.ds(rb, chunk), sl],
                lb_ref[1:2, sl], stb_scr.at[hd], True, row, code_b)
        return carry

    lax.fori_loop(0, n_chunks, body, 0)

    def epilogue(i, carry):
        r = pl.multiple_of(i * chunk, chunk)
        for hd in range(n_heads):
            sl = slice(hd * HEAD_DIM, (hd + 1) * HEAD_DIM)
            o = of_scr[pl.ds(r, chunk), sl] + ob_scr[pl.ds(r, chunk), sl]
            y = o * lax.rsqrt(jnp.mean(o * o, axis=-1, keepdims=True) + RMS_EPS) * nw_ref[:, sl]
            g = g_ref[pl.ds(r, chunk), sl]
            o_ref[pl.ds(r, chunk), sl] = (y * (g * _sigmoid(g))).astype(o_ref.dtype)
        return carry

    lax.fori_loop(0, n_chunks, epilogue, 0)


def _hgrn2(u3, lb, norm_w, a_w):
    bsz, s_len, _ = u3.shape
    heads = a_w // HEAD_DIM
    chunk = min(HG_CHUNK, s_len)
    hps = HG_HEADS_PER_STEP
    wid = hps * HEAD_DIM
    groups = heads // hps

    def col(k):
        return pl.BlockSpec((None, s_len, wid), lambda b, h: (b, 0, k * groups + h))

    out = pl.pallas_call(
        functools.partial(_hgrn_kernel, chunk=chunk),
        grid=(bsz, groups),
        in_specs=[col(0), col(1), col(2), col(3), col(4),
                  pl.BlockSpec((2, wid), lambda b, h: (0, h)),
                  pl.BlockSpec((1, wid), lambda b, h: (0, h))],
        out_specs=pl.BlockSpec((None, s_len, wid), lambda b, h: (b, 0, h)),
        out_shape=jax.ShapeDtypeStruct((bsz, s_len, a_w), BF16),
        scratch_shapes=[pltpu.VMEM((s_len, wid), F32), pltpu.VMEM((s_len, wid), F32),
                        pltpu.VMEM((hps, HEAD_DIM, HEAD_DIM), F32), pltpu.VMEM((hps, HEAD_DIM, HEAD_DIM), F32)],
        compiler_params=_params(("parallel", "arbitrary"), 14 * s_len * wid * 4 + (16 << 20)),
        name="hgrn2",
    )(u3, u3, u3, u3, u3, lb, norm_w.reshape(1, a_w))
    return out.reshape(bsz * s_len, a_w)


def _gelu_tanh(x):
    return 0.5 * x * (1.0 + jnp.tanh(math.sqrt(2.0 / math.pi) * (x + 0.044715 * (x * x * x))))


def _rglru_kernel(x_ref, xp_ref, xn_ref, cw_ref, cb_ref, w_ref, bias_ref, lam_ref, *rest, rev):
    if rev:
        gate_ref, hf_ref, o_ref, carry_scr = rest
    else:
        o_ref, carry_scr = rest
    ts, width = x_ref.shape
    n_grp = ts // SUBLANES
    step = pl.program_id(1)
    tile = pl.num_programs(1) - 1 - step if rev else step

    @pl.when(step == 0)
    def _():
        carry_scr[...] = jnp.zeros_like(carry_scr)

    has_prev = (tile > 0).astype(F32)
    has_next = (tile < pl.num_programs(1) - 1).astype(F32)
    x = x_ref[...]
    row = lax.broadcasted_iota(jnp.int32, (ts, width), 0)
    p6 = xp_ref[SUBLANES - 2:SUBLANES - 1, :] * has_prev
    p7 = xp_ref[SUBLANES - 1:SUBLANES, :] * has_prev
    n0 = xn_ref[0:1, :] * has_next
    xm1 = jnp.where(row == 0, p7, pltpu.roll(x, 1, 0))
    xm2 = jnp.where(row == 0, p6, jnp.where(row == 1, p7, pltpu.roll(x, 2, 0)))
    xp1 = jnp.where(row == ts - 1, n0, pltpu.roll(x, ts - 1, 0))
    xc = (cb_ref[...] + cw_ref[0:1, :] * xm2 + cw_ref[1:2, :] * xm1
          + cw_ref[2:3, :] * x + cw_ref[3:4, :] * xp1)
    lam = lam_ref[...]
    sp = jnp.maximum(-lam, 0.0) + jnp.log1p(jnp.exp(-jnp.abs(lam)))
    sub = lax.broadcasted_iota(jnp.int32, (n_grp, SUBLANES, HEAD_DIM), 1)
    for n in range(width // HEAD_DIM):
        sl = slice(n * HEAD_DIM, (n + 1) * HEAD_DIM)
        xn_blk = xc[:, sl]
        zz = _dot(xn_blk.astype(BF16), w_ref[n])
        r = _sigmoid(zz[:, :HEAD_DIM] + bias_ref[0:1, sl])
        ig = _sigmoid(zz[:, HEAD_DIM:] + bias_ref[1:2, sl])
        log_a = (-RG_C) * r * sp[:, sl]
        a = jnp.exp(log_a)
        b = jnp.sqrt(-jnp.tanh(log_a) * (a * a + 1.0)) * (ig * xn_blk)
        a = a.reshape(n_grp, SUBLANES, HEAD_DIM)
        b = b.reshape(n_grp, SUBLANES, HEAD_DIM)
        for d in (1, 2, 4):
            if rev:
                a_s, b_s, ok = pltpu.roll(a, SUBLANES - d, 1), pltpu.roll(b, SUBLANES - d, 1), sub < SUBLANES - d
            else:
                a_s, b_s, ok = pltpu.roll(a, d, 1), pltpu.roll(b, d, 1), sub >= d
            b = jnp.where(ok, a * b_s + b, b)
            a = jnp.where(ok, a * a_s, a)
        h_in = jnp.broadcast_to(carry_scr[:, sl], (SUBLANES, HEAD_DIM))
        last = 0 if rev else SUBLANES - 1
        for g in (range(n_grp - 1, -1, -1) if rev else range(n_grp)):
            rows = slice(g * SUBLANES, (g + 1) * SUBLANES)
            h = a[g] * h_in + b[g]
            h_in = jnp.broadcast_to(h[last:last + 1, :], (SUBLANES, HEAD_DIM))
            if rev:
                o_ref[rows, sl] = (_gelu_tanh(gate_ref[rows, sl]) * (hf_ref[rows, sl] + h)).astype(o_ref.dtype)
            else:
                o_ref[rows, sl] = h
        carry_scr[:, sl] = h_in[0:1, :]


def _rglru_dir(u3, col0, conv_w, conv_b, w_dir, bias_dir, lam_dir, b_w, hf=None):
    bsz, s_len, _ = u3.shape
    rev = hf is not None
    ts = min(RG_ROWS, s_len)
    nt = s_len // ts
    cb = col0 // b_w
    rows8 = ts // SUBLANES

    def tile(i):
        return nt - 1 - i if rev else i

    in_specs = [pl.BlockSpec((None, ts, b_w), lambda b, i: (b, tile(i), cb)),
                pl.BlockSpec((None, SUBLANES, b_w), lambda b, i: (b, jnp.maximum(tile(i) * rows8 - 1, 0), cb)),
                pl.BlockSpec((None, SUBLANES, b_w),
                             lambda b, i: (b, jnp.minimum((tile(i) + 1) * rows8, s_len // SUBLANES - 1), cb)),
                pl.BlockSpec((4, b_w), lambda b, i: (0, 0)),
                pl.BlockSpec((1, b_w), lambda b, i: (0, 0)),
                pl.BlockSpec(w_dir.shape, lambda b, i: (0, 0, 0)),
                pl.BlockSpec((2, b_w), lambda b, i: (0, 0)),
                pl.BlockSpec((1, b_w), lambda b, i: (0, 0))]
    args = [u3, u3, u3, conv_w, conv_b.reshape(1, b_w), w_dir, bias_dir, lam_dir.reshape(1, b_w)]
    if rev:
        in_specs += [pl.BlockSpec((None, ts, b_w), lambda b, i: (b, tile(i), cb + 1)),
                     pl.BlockSpec((None, ts, b_w), lambda b, i: (b, tile(i), 0))]
        args += [u3, hf]
    return pl.pallas_call(
        functools.partial(_rglru_kernel, rev=rev),
        grid=(bsz, nt),
        in_specs=in_specs,
        out_specs=pl.BlockSpec((None, ts, b_w), lambda b, i: (b, tile(i), 0)),
        out_shape=jax.ShapeDtypeStruct((bsz, s_len, b_w), BF16 if rev else F32),
        scratch_shapes=[pltpu.VMEM((1, b_w), F32)],
        compiler_params=_params(("parallel", "arbitrary"), 2 * 4 * ts * b_w * 4 + (24 << 20)),
        name="rglru_bwd" if rev else "rglru_fwd",
    )(*args)


def _attn_prep_kernel(q0_ref, q1_ref, kv_ref, cos_ref, sa_ref, sb_ref, qn_ref, kn_ref,
                      qo_ref, ko_ref, vo_ref):
    cos = cos_ref[...]
    sin_a = sa_ref[...]
    sin_b = sb_ref[...]

    def norm_rope(t, w):
        y = t * lax.rsqrt(jnp.mean(t * t, axis=-1, keepdims=True) + RMS_EPS) * w
        return (y * cos + pltpu.roll(y, HEAD_DIM - ROPE_HALF // 2, 1) * sin_a
                + pltpu.roll(y, ROPE_HALF // 2, 1) * sin_b)

    half_heads = q0_ref.shape[1] // HEAD_DIM
    kv_heads = ko_ref.shape[1] // HEAD_DIM
    qn = qn_ref[...]
    kn = kn_ref[...]
    for hd in range(half_heads):
        sl = slice(hd * HEAD_DIM, (hd + 1) * HEAD_DIM)
        so = slice((half_heads + hd) * HEAD_DIM, (half_heads + hd + 1) * HEAD_DIM)
        qo_ref[:, sl] = norm_rope(q0_ref[:, sl], qn).astype(BF16)
        qo_ref[:, so] = norm_rope(q1_ref[:, sl], qn).astype(BF16)
    for hd in range(kv_heads):
        sl = slice(hd * HEAD_DIM, (hd + 1) * HEAD_DIM)
        ko_ref[:, sl] = norm_rope(kv_ref[:, sl], kn).astype(BF16)
    vo_ref[...] = kv_ref[:, kv_heads * HEAD_DIM:].T.astype(BF16)


def _attn_prep(u2, col0, c_w, kv_w, s_len, cos, sin_a, sin_b, qn_w, kn_w):
    t = u2.shape[0]
    tr = min(PREP_ROWS, s_len)
    half = c_w // 2
    assert col0 % half == 0 and 2 * kv_w == half
    cb = col0 // half
    npos = s_len // tr
    tab = pl.BlockSpec((tr, HEAD_DIM), lambda i: (i % npos, 0))
    vec = pl.BlockSpec((1, HEAD_DIM), lambda i: (0, 0))
    return pl.pallas_call(
        _attn_prep_kernel,
        grid=(t // tr,),
        in_specs=[pl.BlockSpec((tr, half), lambda i: (i, cb)),
                  pl.BlockSpec((tr, half), lambda i: (i, cb + 1)),
                  pl.BlockSpec((tr, half), lambda i: (i, cb + 2)),
                  tab, tab, tab, vec, vec],
        out_specs=[pl.BlockSpec((tr, c_w), lambda i: (i, 0)),
                   pl.BlockSpec((tr, kv_w), lambda i: (i, 0)),
                   pl.BlockSpec((kv_w, tr), lambda i: (0, i))],
        out_shape=[jax.ShapeDtypeStruct((t, c_w), BF16),
                   jax.ShapeDtypeStruct((t, kv_w), BF16),
                   jax.ShapeDtypeStruct((kv_w, t), BF16)],
        compiler_params=_params(("parallel",), 2 * tr * (3 * half * 4 + (c_w + 2 * kv_w) * 2) + (16 << 20)),
        name="attn_prep",
    )(u2, u2, u2, cos, sin_a, sin_b, qn_w.reshape(1, HEAD_DIM), kn_w.reshape(1, HEAD_DIM))


def _attn_kernel(q_ref, k_ref, vt_ref, o_ref, sc_scr):
    k = k_ref[...]
    vt = vt_ref[...]
    vt_aug = jnp.concatenate([vt, jnp.ones((BF16_ROWS, vt.shape[1]), BF16)], axis=0)
    s_len = k.shape[0]
    kc = min(ATTN_KB, s_len)
    n_kc = s_len // kc

    def scores(g, c, run_max):
        rows = slice(c * kc, (c + 1) * kc)
        sc = _dot_nt(k[rows, :], q_ref[:, g * HEAD_DIM:(g + 1) * HEAD_DIM])
        sc_scr[g % 2, rows, :] = sc
        cmax = jnp.max(sc.reshape(kc // SUBLANES, SUBLANES, sc.shape[1]), axis=0)
        return cmax if run_max is None else jnp.maximum(run_max, cmax)

    next_max = None
    for c in range(n_kc):
        next_max = scores(0, c, next_max)
    for g in range(C_GROUP):
        mx = jnp.max(next_max, axis=0, keepdims=True)
        next_max = None
        acc = None
        for c in range(n_kc):
            rows = slice(c * kc, (c + 1) * kc)
            p = jnp.exp2((sc_scr[g % 2, rows, :] - mx) * (ATTN_SCALE * LOG2_E)).astype(BF16)
            if g + 1 < C_GROUP:
                next_max = scores(g + 1, c, next_max)
            part = _dot(vt_aug[:, rows], p)
            acc = part if acc is None else acc + part
        ot = acc[:HEAD_DIM, :] / acc[HEAD_DIM:HEAD_DIM + 1, :]
        o_ref[:, g * HEAD_DIM:(g + 1) * HEAD_DIM] = ot.T.astype(o_ref.dtype)


def _attention(qr, kr, vt, bsz, s_len):
    t, c_w = qr.shape
    kv_heads = kr.shape[1] // HEAD_DIM
    gw = C_GROUP * HEAD_DIM
    tq = min(ATTN_TQ, s_len)
    nq = s_len // tq
    return pl.pallas_call(
        _attn_kernel,
        grid=(bsz, kv_heads, nq),
        in_specs=[pl.BlockSpec((tq, gw), lambda b, h, i: (b * nq + i, h)),
                  pl.BlockSpec((s_len, HEAD_DIM), lambda b, h, i: (b, h)),
                  pl.BlockSpec((HEAD_DIM, s_len), lambda b, h, i: (h, b))],
        out_specs=pl.BlockSpec((tq, gw), lambda b, h, i: (b * nq + i, h)),
        out_shape=jax.ShapeDtypeStruct((t, c_w), BF16),
        scratch_shapes=[pltpu.VMEM((2, s_len, tq), F32)],
        compiler_params=_params(("parallel", "parallel", "arbitrary"),
                                8 * tq * s_len * 4 + 4 * s_len * HEAD_DIM * 2 + (8 << 20)),
        name="attention",
    )(qr, kr, vt)


def _outproj_kernel(ya_ref, yb_ref, yc_ref, wa_ref, wb_ref, wc_ref, res_ref, o_ref, *, alpha):
    acc = _dot(ya_ref[...], wa_ref[...])
    acc = acc + _dot(yb_ref[...], wb_ref[...])
    acc = acc + _dot(yc_ref[...], wc_ref[...])
    o_ref[...] = alpha * res_ref[...] + acc


def _outproj(ya, yb, yc, w_out, layer, res, alpha):
    t, a_w = ya.shape
    b_w = yb.shape[1]
    c_w = yc.shape[1]
    n = w_out.shape[2]
    assert a_w == b_w and (a_w + b_w) % c_w == 0
    tm = min(OUT_TM, t)
    tn = min(MM_TN, n)
    c_blk = (a_w + b_w) // c_w
    vm = 2 * (tm * (a_w + b_w + c_w) * 2 + (a_w + b_w + c_w) * tn * 2 + 2 * tm * tn * 4) + tm * tn * 4 + (4 << 20)
    return pl.pallas_call(
        functools.partial(_outproj_kernel, alpha=alpha),
        grid=(t // tm, n // tn),
        in_specs=[pl.BlockSpec((tm, a_w), lambda i, j: (i, 0)),
                  pl.BlockSpec((tm, b_w), lambda i, j: (i, 0)),
                  pl.BlockSpec((tm, c_w), lambda i, j: (i, 0)),
                  pl.BlockSpec((None, a_w, tn), lambda i, j: (layer, 0, j)),
                  pl.BlockSpec((None, b_w, tn), lambda i, j: (layer, 1, j)),
                  pl.BlockSpec((None, c_w, tn), lambda i, j: (layer, c_blk, j)),
                  pl.BlockSpec((tm, tn), lambda i, j: (i, j))],
        out_specs=pl.BlockSpec((tm, tn), lambda i, j: (i, j)),
        out_shape=jax.ShapeDtypeStruct((t, n), F32),
        compiler_params=_params(("parallel", "arbitrary"), vm),
        name="outproj",
    )(ya, yb, yc, w_out, w_out, w_out, res)


def _ffn_up_kernel(x_ref, ex_ref, wg_ref, wu_ref, cw_ref, cb_ref, o_ref, wg_scr, wu_scr, edge_scr,
                   *, tiles_per_seq):
    tm, tn = o_ref.shape
    i = pl.program_id(1)

    @pl.when(i == 0)
    def _():
        wg_scr[...] = wg_ref[...].astype(BF16)
        wu_scr[...] = wu_ref[...].astype(BF16)
        edge_scr[...] = _dot(ex_ref[...], wg_scr[...])

    x = x_ref[...]
    g = _dot(x, wg_scr[...])
    up = _dot(x, wu_scr[...])
    pos = i % tiles_per_seq
    has_prev = (pos > 0).astype(F32)
    has_next = (pos < tiles_per_seq - 1).astype(F32)
    last_edge = edge_scr.shape[0] - 1
    g_prev = edge_scr[pl.ds(jnp.maximum(2 * i - 1, 0), 1), :] * has_prev
    g_next = edge_scr[pl.ds(jnp.minimum(2 * i + 2, last_edge), 1), :] * has_next
    row = lax.broadcasted_iota(jnp.int32, (tm, tn), 0)
    g_m1 = jnp.where(row == 0, g_prev, pltpu.roll(g, 1, 0))
    g_p1 = jnp.where(row == tm - 1, g_next, pltpu.roll(g, tm - 1, 0))
    y = cb_ref[...] + cw_ref[0:1, :] * g_m1 + cw_ref[1:2, :] * g + cw_ref[2:3, :] * g_p1
    o_ref[...] = ((y * _sigmoid(y)) * up).astype(o_ref.dtype)


def _ffn_up(hbf, w_up, layer, conv_w, conv_b, s_len):
    t, d = hbf.shape
    d_ff = w_up.shape[2] // 2
    tm = min(FFN_TM, s_len)
    tn = FFN_TN
    assert d_ff % tn == 0
    nj = d_ff // tn
    n_tiles = t // tm
    tiles_per_seq = s_len // tm
    h3 = hbf.reshape(n_tiles, tm, d)
    edge_x = jnp.stack([h3[:, 0, :], h3[:, tm - 1, :]], axis=1).reshape(2 * n_tiles, d)
    vm = 2 * (tm * d * 2 + 2 * d * tn * 4 + tm * tn * 2) + 2 * d * tn * 2 + 8 * tm * tn * 4 + (4 << 20)
    w_scr = pltpu.VMEM((d, tn), BF16)
    return pl.pallas_call(
        functools.partial(_ffn_up_kernel, tiles_per_seq=tiles_per_seq),
        grid=(nj, n_tiles),
        in_specs=[pl.BlockSpec((tm, d), lambda j, i: (i, 0)),
                  pl.BlockSpec((2 * n_tiles, d), lambda j, i: (0, 0)),
                  pl.BlockSpec((None, d, tn), lambda j, i: (layer, 0, j)),
                  pl.BlockSpec((None, d, tn), lambda j, i: (layer, 0, nj + j)),
                  pl.BlockSpec((3, tn), lambda j, i: (0, j)),
                  pl.BlockSpec((1, tn), lambda j, i: (0, j))],
        out_specs=pl.BlockSpec((tm, tn), lambda j, i: (i, j)),
        out_shape=jax.ShapeDtypeStruct((t, d_ff), BF16),
        scratch_shapes=[w_scr, w_scr, pltpu.VMEM((2 * n_tiles, tn), F32)],
        compiler_params=_params(("parallel", "arbitrary"), vm),
        name="ffn_up",
    )(hbf, edge_x, w_up, w_up, conv_w, conv_b.reshape(1, d_ff))


def _rope_tables(s_len):
    rows = s_len // GRID_W
    g_r, g_c = jnp.meshgrid(jnp.arange(rows), jnp.arange(GRID_W), indexing='ij')
    row = g_r.reshape(s_len).astype(F32)
    colp = g_c.reshape(s_len).astype(F32)
    inv_freq = ROPE_THETA ** (-jnp.arange(0, ROPE_HALF, 2, dtype=F32) / ROPE_HALF)
    ang_r = row[:, None] * inv_freq[None, :]
    ang_c = colp[:, None] * inv_freq[None, :]
    ang = jnp.concatenate([ang_r, ang_r, ang_c, ang_c], axis=-1)
    cos, sin = jnp.cos(ang), jnp.sin(ang)
    first_quarter = (jnp.arange(HEAD_DIM) & (ROPE_HALF // 2)) == 0
    sin_a = jnp.where(first_quarter, -sin, 0.0)
    sin_b = jnp.where(first_quarter, 0.0, sin)
    return cos, sin_a, sin_b


def kernel(x, emb_ln_w, emb_ln_b, w_in, hgrn_lb_logits, hgrn_norm_w, rglru_conv_w, rglru_conv_b,
           rglru_wa, rglru_ba, rglru_wx, rglru_bx, rglru_lambda, attn_q_norm_w, attn_k_norm_w,
           w_out, ln1_w, ln1_b, ffn_w_up, ffn_conv_w, ffn_conv_b, ffn_w_down, ln2_w, ln2_b):
    bsz, s_len, d_model = x.shape
    depth = w_in.shape[0]
    t = bsz * s_len
    a_w = hgrn_norm_w.shape[1]
    b_w = rglru_conv_w.shape[2]
    in_cols = w_in.shape[2]
    a_cols = 5 * a_w
    b_cols = 2 * b_w
    c_w = w_out.shape[1] - a_w - b_w
    kv_w = (in_cols - a_cols - b_cols - c_w) // 2
    alpha = (2.0 * depth) ** 0.25

    cos, sin_a, sin_b = _rope_tables(s_len)
    lb_cs = jnp.cumsum(jax.nn.softmax(hgrn_lb_logits.astype(F32), axis=0), axis=0)
    lower_bounds = lb_cs - lb_cs[0:1]

    w_out_bf = w_out.astype(BF16)
    w_down_bf = ffn_w_down.astype(BF16)

    h, hbf = _layernorm(x.reshape(t, d_model), emb_ln_w, emb_ln_b)
    for l in range(depth):
        u2 = _matmul_f32_weights(hbf, w_in, l, MM_TM, IN_TN, "in_proj")
        u3 = u2.reshape(bsz, s_len, in_cols)

        ya = _hgrn2(u3, lower_bounds[l], hgrn_norm_w[l], a_w)

        yb = None
        for d in range(2):
            w_dir = jnp.concatenate([rglru_wa[l, d], rglru_wx[l, d]], axis=-1).astype(BF16)
            bias_dir = jnp.stack([rglru_ba[l, d], rglru_bx[l, d]])
            yb = _rglru_dir(u3, a_cols, rglru_conv_w[l], rglru_conv_b[l], w_dir, bias_dir,
                            rglru_lambda[l, d], b_w, hf=yb)
        yb = yb.reshape(t, b_w)

        qr, kr, vt = _attn_prep(u2, a_cols + b_cols, c_w, kv_w, s_len, cos, sin_a, sin_b,
                                attn_q_norm_w[l], attn_k_norm_w[l])
        yc = _attention(qr, kr, vt, bsz, s_len)

        pre = _outproj(ya, yb, yc, w_out_bf, l, h, alpha)
        h, hbf = _layernorm(pre, ln1_w[l], ln1_b[l])

        act = _ffn_up(hbf, ffn_w_up, l, ffn_conv_w[l], ffn_conv_b[l], s_len)
        pre = _matmul_residual(act, w_down_bf, l, h, alpha, DOWN_TM, DOWN_TN, "ffn_down")
        h, hbf = _layernorm(pre, ln2_w[l], ln2_b[l])
    return h.reshape(bsz, s_len, d_model)
```

```python
import functools
import math

import jax
import jax.numpy as jnp
from jax import lax
from jax.experimental import pallas as pl
from jax.experimental.pallas import tpu as pltpu

F32 = jnp.float32
BF16 = jnp.bfloat16

HEAD_DIM = 128
RG_C = 8.0
C_GROUP = 4
ROPE_THETA = 10000.0
ROPE_HALF = HEAD_DIM // 2
GRID_W = 64
LN_EPS = 1e-5
RMS_EPS = 1e-6
ATTN_SCALE = HEAD_DIM ** -0.5
LOG2_E = math.log2(math.e)

V7X_VMEM_BYTES = 64 * 1024 * 1024
SUBLANES = 8
BF16_ROWS = 16

HG_CHUNK = 256
HG_HEADS_PER_STEP = 4
LN_ROWS = 256
MM_TM = 1024
MM_TN = 1024
OUT_TM = 1024
FFN_TM = 1024
FFN_TN = 256
DOWN_TM = 512
DOWN_TN = 512
ATTN_TQ = 2048
ATTN_KB = 1024
PREP_ROWS = 1024
RG_ROWS = 1024


def _vmem_limit(nbytes):
    return int(min(V7X_VMEM_BYTES - 4 * 1024 * 1024, max(nbytes, 16 * 1024 * 1024)))


def _params(sem, vmem_bytes):
    return pltpu.CompilerParams(dimension_semantics=sem, vmem_limit_bytes=_vmem_limit(vmem_bytes))


def _dot(a, b):
    return jnp.dot(a, b, preferred_element_type=F32)


def _dot_nt(a, b):
    return lax.dot_general(a, b, (((1,), (1,)), ((), ())), preferred_element_type=F32)


def _dot_tn(a, b):
    return lax.dot_general(a, b, (((0,), (0,)), ((), ())), preferred_element_type=F32)


def _sigmoid(x):
    return 1.0 / (1.0 + jnp.exp(-x))


def _ln_kernel(x_ref, w_ref, b_ref, o_ref, obf_ref):
    x = x_ref[...]
    mu = jnp.mean(x, axis=-1, keepdims=True)
    xc = x - mu
    var = jnp.mean(xc * xc, axis=-1, keepdims=True)
    y = xc * lax.rsqrt(var + LN_EPS) * w_ref[...] + b_ref[...]
    o_ref[...] = y
    obf_ref[...] = y.astype(BF16)


def _layernorm(x, w, b):
    t, d = x.shape
    tr = min(LN_ROWS, t)
    return pl.pallas_call(
        _ln_kernel,
        grid=(t // tr,),
        in_specs=[pl.BlockSpec((tr, d), lambda i: (i, 0)),
                  pl.BlockSpec((1, d), lambda i: (0, 0)),
                  pl.BlockSpec((1, d), lambda i: (0, 0))],
        out_specs=[pl.BlockSpec((tr, d), lambda i: (i, 0)),
                   pl.BlockSpec((tr, d), lambda i: (i, 0))],
        out_shape=[jax.ShapeDtypeStruct((t, d), F32), jax.ShapeDtypeStruct((t, d), BF16)],
        compiler_params=_params(("parallel",), 2 * tr * d * 10 + (8 << 20)),
        name="layernorm",
    )(x, w.reshape(1, d), b.reshape(1, d))


def _mm_kernel(x_ref, w_ref, o_ref):
    o_ref[...] = _dot(x_ref[...], w_ref[...]).astype(o_ref.dtype)


def _matmul(x, w, layer, tm, tn, out_dtype, name):
    m, k = x.shape
    n = w.shape[2]
    tm, tn = min(tm, m), min(tn, n)
    vm = 2 * (tm * k * 2 + k * tn * 2 + tm * tn * 4) + tm * tn * 4 + (4 << 20)
    return pl.pallas_call(
        _mm_kernel,
        grid=(m // tm, n // tn),
        in_specs=[pl.BlockSpec((tm, k), lambda i, j: (i, 0)),
                  pl.BlockSpec((None, k, tn), lambda i, j: (layer, 0, j))],
        out_specs=pl.BlockSpec((tm, tn), lambda i, j: (i, j)),
        out_shape=jax.ShapeDtypeStruct((m, n), out_dtype),
        compiler_params=_params(("parallel", "arbitrary"), vm),
        name=name,
    )(x, w)


def _mm_res_kernel(x_ref, w_ref, r_ref, o_ref, *, alpha):
    o_ref[...] = alpha * r_ref[...] + _dot(x_ref[...], w_ref[...])


def _matmul_residual(x, w, layer, res, alpha, tm, tn, name):
    m, k = x.shape
    n = w.shape[2]
    tm, tn = min(tm, m), min(tn, n)
    vm = 2 * (tm * k * 2 + k * tn * 2 + 2 * tm * tn * 4) + tm * tn * 4 + (4 << 20)
    return pl.pallas_call(
        functools.partial(_mm_res_kernel, alpha=alpha),
        grid=(m // tm, n // tn),
        in_specs=[pl.BlockSpec((tm, k), lambda i, j: (i, 0)),
                  pl.BlockSpec((None, k, tn), lambda i, j: (layer, 0, j)),
                  pl.BlockSpec((tm, tn), lambda i, j: (i, j))],
        out_specs=pl.BlockSpec((tm, tn), lambda i, j: (i, j)),
        out_shape=jax.ShapeDtypeStruct((m, n), F32),
        compiler_params=_params(("parallel", "arbitrary"), vm),
        name=name,
    )(x, w, res)


def _hgrn_chunk(q, z, v, lb, st_ref, rev, row, pair_code):
    c_len = q.shape[0]
    n_lvl = c_len.bit_length() - 1
    t = jnp.exp(-jnp.abs(z))
    r = 1.0 / (1.0 + t)
    tr = t * r
    pos = z >= 0.0
    f = lb + (1.0 - lb) * jnp.where(pos, r, tr)
    kk = (1.0 - lb) * jnp.where(pos, tr, r)

    vb = v.astype(BF16)
    scores = _dot_nt(q.astype(BF16), kk.astype(BF16))
    ep = f
    ex = jnp.ones_like(f)
    et = f
    for lvl in range(n_lvl):
        m = 1 << lvl
        later = ((row & m) == 0) if rev else ((row & m) != 0)
        x = jnp.where(later, q * ep, kk * ex).astype(BF16)
        scores = jnp.where(pair_code >= m, _dot_nt(x, x), scores)
        from_earlier, from_later = (c_len - m, m) if rev else (m, c_len - m)
        sib = jnp.where(later, pltpu.roll(et, from_earlier, 0), pltpu.roll(et, from_later, 0))
        ep = jnp.where(later, ep * sib, ep)
        ex = jnp.where(later, ex, ex * sib)
        et = et * sib
    scores = jnp.where(pair_code < 0, 0.0, scores)

    st = st_ref[...]
    o = _dot_nt((q * ep).astype(BF16), st.astype(BF16)) + _dot(scores.astype(BF16), vb)
    st_ref[...] = et[0:1, :] * st + _dot_tn(vb, (kk * ex).astype(BF16))
    return o


def _hgrn_kernel(q_ref, zf_ref, zb_ref, v_ref, g_ref, lb_ref, nw_ref, o_ref,
                 of_scr, ob_scr, stf_scr, stb_scr, *, chunk):
    s_len = q_ref.shape[0]
    n_chunks = s_len // chunk
    row = lax.broadcasted_iota(jnp.int32, (chunk, HEAD_DIM), 0)
    ri = lax.broadcasted_iota(jnp.int32, (chunk, chunk), 0)
    ci = lax.broadcasted_iota(jnp.int32, (chunk, chunk), 1)
    code_f = jnp.where(ri >= ci, ri ^ ci, -1)
    code_b = jnp.where(ci >= ri, ri ^ ci, -1)
    n_heads = q_ref.shape[1] // HEAD_DIM
    stf_scr[...] = jnp.zeros_like(stf_scr)
    stb_scr[...] = jnp.zeros_like(stb_scr)

    def body(i, carry):
        rf = pl.multiple_of(i * chunk, chunk)
        rb = pl.multiple_of((n_chunks - 1 - i) * chunk, chunk)
        for hd in range(n_heads):
            sl = slice(hd * HEAD_DIM, (hd + 1) * HEAD_DIM)
            of_scr[pl.ds(rf, chunk), sl] = _hgrn_chunk(
                q_ref[pl.ds(rf, chunk), sl], zf_ref[pl.ds(rf, chunk), sl], v_ref[pl.ds(rf, chunk), sl],
                lb_ref[0:1, sl], stf_scr.at[hd], False, row, code_f)
            ob_scr[pl.ds(rb, chunk), sl] = _hgrn_chunk(
                q_ref[pl.ds(rb, chunk), sl], zb_ref[pl.ds(rb, chunk), sl], v_ref[pl.ds(rb, chunk), sl],
                lb_ref[1:2, sl], stb_scr.at[hd], True, row, code_b)
        return carry

    lax.fori_loop(0, n_chunks, body, 0)

    def epilogue(i, carry):
        r = pl.multiple_of(i * chunk, chunk)
        for hd in range(n_heads):
            sl = slice(hd * HEAD_DIM, (hd + 1) * HEAD_DIM)
            o = of_scr[pl.ds(r, chunk), sl] + ob_scr[pl.ds(r, chunk), sl]
            y = o * lax.rsqrt(jnp.mean(o * o, axis=-1, keepdims=True) + RMS_EPS) * nw_ref[:, sl]
            g = g_ref[pl.ds(r, chunk), sl]
            o_ref[pl.ds(r, chunk), sl] = (y * (g * _sigmoid(g))).astype(o_ref.dtype)
        return carry

    lax.fori_loop(0, n_chunks, epilogue, 0)


def _hgrn2(u3, lb, norm_w, a_w):
    bsz, s_len, _ = u3.shape
    heads = a_w // HEAD_DIM
    chunk = min(HG_CHUNK, s_len)
    hps = HG_HEADS_PER_STEP
    wid = hps * HEAD_DIM
    groups = heads // hps

    def col(k):
        return pl.BlockSpec((None, s_len, wid), lambda b, h: (b, 0, k * groups + h))

    out = pl.pallas_call(
        functools.partial(_hgrn_kernel, chunk=chunk),
        grid=(bsz, groups),
        in_specs=[col(0), col(1), col(2), col(3), col(4),
                  pl.BlockSpec((2, wid), lambda b, h: (0, h)),
                  pl.BlockSpec((1, wid), lambda b, h: (0, h))],
        out_specs=pl.BlockSpec((None, s_len, wid), lambda b, h: (b, 0, h)),
        out_shape=jax.ShapeDtypeStruct((bsz, s_len, a_w), BF16),
        scratch_shapes=[pltpu.VMEM((s_len, wid), F32), pltpu.VMEM((s_len, wid), F32),
                        pltpu.VMEM((hps, HEAD_DIM, HEAD_DIM), F32), pltpu.VMEM((hps, HEAD_DIM, HEAD_DIM), F32)],
        compiler_params=_params(("parallel", "arbitrary"), 14 * s_len * wid * 4 + (16 << 20)),
        name="hgrn2",
    )(u3, u3, u3, u3, u3, lb, norm_w.reshape(1, a_w))
    return out.reshape(bsz * s_len, a_w)


def _gelu_tanh(x):
    return 0.5 * x * (1.0 + jnp.tanh(math.sqrt(2.0 / math.pi) * (x + 0.044715 * (x * x * x))))


def _rglru_kernel(x_ref, xp_ref, xn_ref, cw_ref, cb_ref, w_ref, bias_ref, lam_ref, *rest, rev):
    if rev:
        gate_ref, hf_ref, o_ref, carry_scr = rest
    else:
        o_ref, carry_scr = rest
    ts, width = x_ref.shape
    n_grp = ts // SUBLANES
    step = pl.program_id(1)
    tile = pl.num_programs(1) - 1 - step if rev else step

    @pl.when(step == 0)
    def _():
        carry_scr[...] = jnp.zeros_like(carry_scr)

    has_prev = (tile > 0).astype(F32)
    has_next = (tile < pl.num_programs(1) - 1).astype(F32)
    x = x_ref[...]
    row = lax.broadcasted_iota(jnp.int32, (ts, width), 0)
    p6 = xp_ref[SUBLANES - 2:SUBLANES - 1, :] * has_prev
    p7 = xp_ref[SUBLANES - 1:SUBLANES, :] * has_prev
    n0 = xn_ref[0:1, :] * has_next
    xm1 = jnp.where(row == 0, p7, pltpu.roll(x, 1, 0))
    xm2 = jnp.where(row == 0, p6, jnp.where(row == 1, p7, pltpu.roll(x, 2, 0)))
    xp1 = jnp.where(row == ts - 1, n0, pltpu.roll(x, ts - 1, 0))
    xc = (cb_ref[...] + cw_ref[0:1, :] * xm2 + cw_ref[1:2, :] * xm1
          + cw_ref[2:3, :] * x + cw_ref[3:4, :] * xp1)
    lam = lam_ref[...]
    sp = jnp.maximum(-lam, 0.0) + jnp.log1p(jnp.exp(-jnp.abs(lam)))
    sub = lax.broadcasted_iota(jnp.int32, (n_grp, SUBLANES, HEAD_DIM), 1)
    for n in range(width // HEAD_DIM):
        sl = slice(n * HEAD_DIM, (n + 1) * HEAD_DIM)
        xn_blk = xc[:, sl]
        zz = _dot(xn_blk.astype(BF16), w_ref[n])
        r = _sigmoid(zz[:, :HEAD_DIM] + bias_ref[0:1, sl])
        ig = _sigmoid(zz[:, HEAD_DIM:] + bias_ref[1:2, sl])
        log_a = (-RG_C) * r * sp[:, sl]
        a = jnp.exp(log_a)
        b = jnp.sqrt(-jnp.tanh(log_a) * (a * a + 1.0)) * (ig * xn_blk)
        a = a.reshape(n_grp, SUBLANES, HEAD_DIM)
        b = b.reshape(n_grp, SUBLANES, HEAD_DIM)
        for d in (1, 2, 4):
            if rev:
                a_s, b_s, ok = pltpu.roll(a, SUBLANES - d, 1), pltpu.roll(b, SUBLANES - d, 1), sub < SUBLANES - d
            else:
                a_s, b_s, ok = pltpu.roll(a, d, 1), pltpu.roll(b, d, 1), sub >= d
            b = jnp.where(ok, a * b_s + b, b)
            a = jnp.where(ok, a * a_s, a)
        h_in = jnp.broadcast_to(carry_scr[:, sl], (SUBLANES, HEAD_DIM))
        last = 0 if rev else SUBLANES - 1
        for g in (range(n_grp - 1, -1, -1) if rev else range(n_grp)):
            rows = slice(g * SUBLANES, (g + 1) * SUBLANES)
            h = a[g] * h_in + b[g]
            h_in = jnp.broadcast_to(h[last:last + 1, :], (SUBLANES, HEAD_DIM))
            if rev:
                o_ref[rows, sl] = (_gelu_tanh(gate_ref[rows, sl]) * (hf_ref[rows, sl] + h)).astype(o_ref.dtype)
            else:
                o_ref[rows, sl] = h
        carry_scr[:, sl] = h_in[0:1, :]


def _rglru_dir(u3, col0, conv_w, conv_b, w_dir, bias_dir, lam_dir, b_w, hf=None):
    bsz, s_len, _ = u3.shape
    rev = hf is not None
    ts = min(RG_ROWS, s_len)
    nt = s_len // ts
    cb = col0 // b_w
    rows8 = ts // SUBLANES

    def tile(i):
        return nt - 1 - i if rev else i

    in_specs = [pl.BlockSpec((None, ts, b_w), lambda b, i: (b, tile(i), cb)),
                pl.BlockSpec((None, SUBLANES, b_w), lambda b, i: (b, jnp.maximum(tile(i) * rows8 - 1, 0), cb)),
                pl.BlockSpec((None, SUBLANES, b_w),
                             lambda b, i: (b, jnp.minimum((tile(i) + 1) * rows8, s_len // SUBLANES - 1), cb)),
                pl.BlockSpec((4, b_w), lambda b, i: (0, 0)),
                pl.BlockSpec((1, b_w), lambda b, i: (0, 0)),
                pl.BlockSpec(w_dir.shape, lambda b, i: (0, 0, 0)),
                pl.BlockSpec((2, b_w), lambda b, i: (0, 0)),
                pl.BlockSpec((1, b_w), lambda b, i: (0, 0))]
    args = [u3, u3, u3, conv_w, conv_b.reshape(1, b_w), w_dir, bias_dir, lam_dir.reshape(1, b_w)]
    if rev:
        in_specs += [pl.BlockSpec((None, ts, b_w), lambda b, i: (b, tile(i), cb + 1)),
                     pl.BlockSpec((None, ts, b_w), lambda b, i: (b, tile(i), 0))]
        args += [u3, hf]
    return pl.pallas_call(
        functools.partial(_rglru_kernel, rev=rev),
        grid=(bsz, nt),
        in_specs=in_specs,
        out_specs=pl.BlockSpec((None, ts, b_w), lambda b, i: (b, tile(i), 0)),
        out_shape=jax.ShapeDtypeStruct((bsz, s_len, b_w), BF16 if rev else F32),
        scratch_shapes=[pltpu.VMEM((1, b_w), F32)],
        compiler_params=_params(("parallel", "arbitrary"), 2 * 4 * ts * b_w * 4 + (24 << 20)),
        name="rglru_bwd" if rev else "rglru_fwd",
    )(*args)


def _attn_prep_kernel(q0_ref, q1_ref, kv_ref, cos_ref, sa_ref, sb_ref, qn_ref, kn_ref,
                      qo_ref, ko_ref, vo_ref):
    cos = cos_ref[...]
    sin_a = sa_ref[...]
    sin_b = sb_ref[...]

    def norm_rope(t, w):
        y = t * lax.rsqrt(jnp.mean(t * t, axis=-1, keepdims=True) + RMS_EPS) * w
        return (y * cos + pltpu.roll(y, HEAD_DIM - ROPE_HALF // 2, 1) * sin_a
                + pltpu.roll(y, ROPE_HALF // 2, 1) * sin_b)

    half_heads = q0_ref.shape[1] // HEAD_DIM
    kv_heads = ko_ref.shape[1] // HEAD_DIM
    qn = qn_ref[...]
    kn = kn_ref[...]
    for hd in range(half_heads):
        sl = slice(hd * HEAD_DIM, (hd + 1) * HEAD_DIM)
        so = slice((half_heads + hd) * HEAD_DIM, (half_heads + hd + 1) * HEAD_DIM)
        qo_ref[:, sl] = norm_rope(q0_ref[:, sl], qn).astype(BF16)
        qo_ref[:, so] = norm_rope(q1_ref[:, sl], qn).astype(BF16)
    for hd in range(kv_heads):
        sl = slice(hd * HEAD_DIM, (hd + 1) * HEAD_DIM)
        ko_ref[:, sl] = norm_rope(kv_ref[:, sl], kn).astype(BF16)
    vo_ref[...] = kv_ref[:, kv_heads * HEAD_DIM:].T.astype(BF16)


def _attn_prep(u2, col0, c_w, kv_w, s_len, cos, sin_a, sin_b, qn_w, kn_w):
    t = u2.shape[0]
    tr = min(PREP_ROWS, s_len)
    half = c_w // 2
    assert col0 % half == 0 and 2 * kv_w == half
    cb = col0 // half
    npos = s_len // tr
    tab = pl.BlockSpec((tr, HEAD_DIM), lambda i: (i % npos, 0))
    vec = pl.BlockSpec((1, HEAD_DIM), lambda i: (0, 0))
    return pl.pallas_call(
        _attn_prep_kernel,
        grid=(t // tr,),
        in_specs=[pl.BlockSpec((tr, half), lambda i: (i, cb)),
                  pl.BlockSpec((tr, half), lambda i: (i, cb + 1)),
                  pl.BlockSpec((tr, half), lambda i: (i, cb + 2)),
                  tab, tab, tab, vec, vec],
        out_specs=[pl.BlockSpec((tr, c_w), lambda i: (i, 0)),
                   pl.BlockSpec((tr, kv_w), lambda i: (i, 0)),
                   pl.BlockSpec((kv_w, tr), lambda i: (0, i))],
        out_shape=[jax.ShapeDtypeStruct((t, c_w), BF16),
                   jax.ShapeDtypeStruct((t, kv_w), BF16),
                   jax.ShapeDtypeStruct((kv_w, t), BF16)],
        compiler_params=_params(("parallel",), 2 * tr * (3 * half * 4 + (c_w + 2 * kv_w) * 2) + (16 << 20)),
        name="attn_prep",
    )(u2, u2, u2, cos, sin_a, sin_b, qn_w.reshape(1, HEAD_DIM), kn_w.reshape(1, HEAD_DIM))


def _attn_kernel(q_ref, k_ref, vt_ref, o_ref, sc_scr):
    k = k_ref[...]
    vt = vt_ref[...]
    vt_aug = jnp.concatenate([vt, jnp.ones((BF16_ROWS, vt.shape[1]), BF16)], axis=0)
    s_len = k.shape[0]
    kc = min(ATTN_KB, s_len)
    n_kc = s_len // kc

    def scores(g, c, run_max):
        rows = slice(c * kc, (c + 1) * kc)
        sc = _dot_nt(k[rows, :], q_ref[:, g * HEAD_DIM:(g + 1) * HEAD_DIM])
        sc_scr[g % 2, rows, :] = sc
        cmax = jnp.max(sc.reshape(kc // SUBLANES, SUBLANES, sc.shape[1]), axis=0)
        return cmax if run_max is None else jnp.maximum(run_max, cmax)

    next_max = None
    for c in range(n_kc):
        next_max = scores(0, c, next_max)
    for g in range(C_GROUP):
        mx = jnp.max(next_max, axis=0, keepdims=True)
        next_max = None
        acc = None
        for c in range(n_kc):
            rows = slice(c * kc, (c + 1) * kc)
            p = jnp.exp2((sc_scr[g % 2, rows, :] - mx) * (ATTN_SCALE * LOG2_E)).astype(BF16)
            if g + 1 < C_GROUP:
                next_max = scores(g + 1, c, next_max)
            part = _dot(vt_aug[:, rows], p)
            acc = part if acc is None else acc + part
        ot = acc[:HEAD_DIM, :] / acc[HEAD_DIM:HEAD_DIM + 1, :]
        o_ref[:, g * HEAD_DIM:(g + 1) * HEAD_DIM] = ot.T.astype(o_ref.dtype)


def _attention(qr, kr, vt, bsz, s_len):
    t, c_w = qr.shape
    kv_heads = kr.shape[1] // HEAD_DIM
    gw = C_GROUP * HEAD_DIM
    tq = min(ATTN_TQ, s_len)
    nq = s_len // tq
    return pl.pallas_call(
        _attn_kernel,
        grid=(bsz, kv_heads, nq),
        in_specs=[pl.BlockSpec((tq, gw), lambda b, h, i: (b * nq + i, h)),
                  pl.BlockSpec((s_len, HEAD_DIM), lambda b, h, i: (b, h)),
                  pl.BlockSpec((HEAD_DIM, s_len), lambda b, h, i: (h, b))],
        out_specs=pl.BlockSpec((tq, gw), lambda b, h, i: (b * nq + i, h)),
        out_shape=jax.ShapeDtypeStruct((t, c_w), BF16),
        scratch_shapes=[pltpu.VMEM((2, s_len, tq), F32)],
        compiler_params=_params(("parallel", "parallel", "arbitrary"),
                                8 * tq * s_len * 4 + 4 * s_len * HEAD_DIM * 2 + (8 << 20)),
        name="attention",
    )(qr, kr, vt)


def _outproj_kernel(ya_ref, yb_ref, yc_ref, wa_ref, wb_ref, wc_ref, res_ref, o_ref, *, alpha):
    acc = _dot(ya_ref[...], wa_ref[...])
    acc = acc + _dot(yb_ref[...], wb_ref[...])
    acc = acc + _dot(yc_ref[...], wc_ref[...])
    o_ref[...] = alpha * res_ref[...] + acc


def _outproj(ya, yb, yc, w_out, layer, res, alpha):
    t, a_w = ya.shape
    b_w = yb.shape[1]
    c_w = yc.shape[1]
    n = w_out.shape[2]
    assert a_w == b_w and (a_w + b_w) % c_w == 0
    tm = min(OUT_TM, t)
    tn = min(MM_TN, n)
    c_blk = (a_w + b_w) // c_w
    vm = 2 * (tm * (a_w + b_w + c_w) * 2 + (a_w + b_w + c_w) * tn * 2 + 2 * tm * tn * 4) + tm * tn * 4 + (4 << 20)
    return pl.pallas_call(
        functools.partial(_outproj_kernel, alpha=alpha),
        grid=(t // tm, n // tn),
        in_specs=[pl.BlockSpec((tm, a_w), lambda i, j: (i, 0)),
                  pl.BlockSpec((tm, b_w), lambda i, j: (i, 0)),
                  pl.BlockSpec((tm, c_w), lambda i, j: (i, 0)),
                  pl.BlockSpec((None, a_w, tn), lambda i, j: (layer, 0, j)),
                  pl.BlockSpec((None, b_w, tn), lambda i, j: (layer, 1, j)),
                  pl.BlockSpec((None, c_w, tn), lambda i, j: (layer, c_blk, j)),
                  pl.BlockSpec((tm, tn), lambda i, j: (i, j))],
        out_specs=pl.BlockSpec((tm, tn), lambda i, j: (i, j)),
        out_shape=jax.ShapeDtypeStruct((t, n), F32),
        compiler_params=_params(("parallel", "arbitrary"), vm),
        name="outproj",
    )(ya, yb, yc, w_out, w_out, w_out, res)


def _ffn_up_kernel(x_ref, ex_ref, wg_ref, wu_ref, cw_ref, cb_ref, o_ref, wg_scr, wu_scr, edge_scr,
                   *, tiles_per_seq):
    tm, tn = o_ref.shape
    i = pl.program_id(1)

    @pl.when(i == 0)
    def _():
        wg_scr[...] = wg_ref[...].astype(BF16)
        wu_scr[...] = wu_ref[...].astype(BF16)
        edge_scr[...] = _dot(ex_ref[...], wg_scr[...])

    x = x_ref[...]
    g = _dot(x, wg_scr[...])
    up = _dot(x, wu_scr[...])
    pos = i % tiles_per_seq
    has_prev = (pos > 0).astype(F32)
    has_next = (pos < tiles_per_seq - 1).astype(F32)
    last_edge = edge_scr.shape[0] - 1
    g_prev = edge_scr[pl.ds(jnp.maximum(2 * i - 1, 0), 1), :] * has_prev
    g_next = edge_scr[pl.ds(jnp.minimum(2 * i + 2, last_edge), 1), :] * has_next
    row = lax.broadcasted_iota(jnp.int32, (tm, tn), 0)
    g_m1 = jnp.where(row == 0, g_prev, pltpu.roll(g, 1, 0))
    g_p1 = jnp.where(row == tm - 1, g_next, pltpu.roll(g, tm - 1, 0))
    y = cb_ref[...] + cw_ref[0:1, :] * g_m1 + cw_ref[1:2, :] * g + cw_ref[2:3, :] * g_p1
    o_ref[...] = ((y * _sigmoid(y)) * up).astype(o_ref.dtype)


def _ffn_up(hbf, w_up, layer, conv_w, conv_b, s_len):
    t, d = hbf.shape
    d_ff = w_up.shape[2] // 2
    tm = min(FFN_TM, s_len)
    tn = FFN_TN
    assert d_ff % tn == 0
    nj = d_ff // tn
    n_tiles = t // tm
    tiles_per_seq = s_len // tm
    h3 = hbf.reshape(n_tiles, tm, d)
    edge_x = jnp.stack([h3[:, 0, :], h3[:, tm - 1, :]], axis=1).reshape(2 * n_tiles, d)
    vm = 2 * (tm * d * 2 + 2 * d * tn * 4 + tm * tn * 2) + 2 * d * tn * 2 + 8 * tm * tn * 4 + (4 << 20)
    w_scr = pltpu.VMEM((d, tn), BF16)
    return pl.pallas_call(
        functools.partial(_ffn_up_kernel, tiles_per_seq=tiles_per_seq),
        grid=(nj, n_tiles),
        in_specs=[pl.BlockSpec((tm, d), lambda j, i: (i, 0)),
                  pl.BlockSpec((2 * n_tiles, d), lambda j, i: (0, 0)),
                  pl.BlockSpec((None, d, tn), lambda j, i: (layer, 0, j)),
                  pl.BlockSpec((None, d, tn), lambda j, i: (layer, 0, nj + j)),
                  pl.BlockSpec((3, tn), lambda j, i: (0, j)),
                  pl.BlockSpec((1, tn), lambda j, i: (0, j))],
        out_specs=pl.BlockSpec((tm, tn), lambda j, i: (i, j)),
        out_shape=jax.ShapeDtypeStruct((t, d_ff), BF16),
        scratch_shapes=[w_scr, w_scr, pltpu.VMEM((2 * n_tiles, tn), F32)],
        compiler_params=_params(("parallel", "arbitrary"), vm),
        name="ffn_up",
    )(hbf, edge_x, w_up, w_up, conv_w, conv_b.reshape(1, d_ff))


def _rope_tables(s_len):
    rows = s_len // GRID_W
    g_r, g_c = jnp.meshgrid(jnp.arange(rows), jnp.arange(GRID_W), indexing='ij')
    row = g_r.reshape(s_len).astype(F32)
    colp = g_c.reshape(s_len).astype(F32)
    inv_freq = ROPE_THETA ** (-jnp.arange(0, ROPE_HALF, 2, dtype=F32) / ROPE_HALF)
    ang_r = row[:, None] * inv_freq[None, :]
    ang_c = colp[:, None] * inv_freq[None, :]
    ang = jnp.concatenate([ang_r, ang_r, ang_c, ang_c], axis=-1)
    cos, sin = jnp.cos(ang), jnp.sin(ang)
    first_quarter = (jnp.arange(HEAD_DIM) & (ROPE_HALF // 2)) == 0
    sin_a = jnp.where(first_quarter, -sin, 0.0)
    sin_b = jnp.where(first_quarter, 0.0, sin)
    return cos, sin_a, sin_b


def kernel(x, emb_ln_w, emb_ln_b, w_in, hgrn_lb_logits, hgrn_norm_w, rglru_conv_w, rglru_conv_b,
           rglru_wa, rglru_ba, rglru_wx, rglru_bx, rglru_lambda, attn_q_norm_w, attn_k_norm_w,
           w_out, ln1_w, ln1_b, ffn_w_up, ffn_conv_w, ffn_conv_b, ffn_w_down, ln2_w, ln2_b):
    bsz, s_len, d_model = x.shape
    depth = w_in.shape[0]
    t = bsz * s_len
    a_w = hgrn_norm_w.shape[1]
    b_w = rglru_conv_w.shape[2]
    in_cols = w_in.shape[2]
    a_cols = 5 * a_w
    b_cols = 2 * b_w
    c_w = w_out.shape[1] - a_w - b_w
    kv_w = (in_cols - a_cols - b_cols - c_w) // 2
    alpha = (2.0 * depth) ** 0.25

    cos, sin_a, sin_b = _rope_tables(s_len)
    lb_cs = jnp.cumsum(jax.nn.softmax(hgrn_lb_logits.astype(F32), axis=0), axis=0)
    lower_bounds = lb_cs - lb_cs[0:1]

    w_in_bf = w_in.astype(BF16)
    w_out_bf = w_out.astype(BF16)
    w_down_bf = ffn_w_down.astype(BF16)

    h, hbf = _layernorm(x.reshape(t, d_model), emb_ln_w, emb_ln_b)
    for l in range(depth):
        u2 = _matmul(hbf, w_in_bf, l, MM_TM, MM_TN, F32, "in_proj")
        u3 = u2.reshape(bsz, s_len, in_cols)

        ya = _hgrn2(u3, lower_bounds[l], hgrn_norm_w[l], a_w)

        yb = None
        for d in range(2):
            w_dir = jnp.concatenate([rglru_wa[l, d], rglru_wx[l, d]], axis=-1).astype(BF16)
            bias_dir = jnp.stack([rglru_ba[l, d], rglru_bx[l, d]])
            yb = _rglru_dir(u3, a_cols, rglru_conv_w[l], rglru_conv_b[l], w_dir, bias_dir,
                            rglru_lambda[l, d], b_w, hf=yb)
        yb = yb.reshape(t, b_w)

        qr, kr, vt = _attn_prep(u2, a_cols + b_cols, c_w, kv_w, s_len, cos, sin_a, sin_b,
                                attn_q_norm_w[l], attn_k_norm_w[l])
        yc = _attention(qr, kr, vt, bsz, s_len)

        pre = _outproj(ya, yb, yc, w_out_bf, l, h, alpha)
        h, hbf = _layernorm(pre, ln1_w[l], ln1_b[l])

        act = _ffn_up(hbf, ffn_w_up, l, ffn_conv_w[l], ffn_conv_b[l], s_len)
        pre = _matmul_residual(act, w_down_bf, l, h, alpha, DOWN_TM, DOWN_TN, "ffn_down")
        h, hbf = _layernorm(pre, ln2_w[l], ln2_b[l])
    return h.reshape(bsz, s_len, d_model)
```

```python
import functools
import math

import jax
import jax.numpy as jnp
from jax import lax
from jax.experimental import pallas as pl
from jax.experimental.pallas import tpu as pltpu

F32 = jnp.float32
BF16 = jnp.bfloat16

HEAD_DIM = 128
RG_C = 8.0
C_GROUP = 4
ROPE_THETA = 10000.0
ROPE_HALF = HEAD_DIM // 2
GRID_W = 64
LN_EPS = 1e-5
RMS_EPS = 1e-6
ATTN_SCALE = HEAD_DIM ** -0.5
LOG2_E = math.log2(math.e)

V7X_VMEM_BYTES = 64 * 1024 * 1024
SUBLANES = 8
BF16_ROWS = 16

HG_CHUNK = 256
HG_HEADS_PER_STEP = 4
LN_ROWS = 512
MM_TM = 1024
MM_TN = 1024
OUT_TM = 1024
FFN_TM = 1024
FFN_TN = 256
DOWN_TM = 512
DOWN_TN = 512
ATTN_TQ = 2048
ATTN_KB = 1024
PREP_ROWS = 1024
RG_ROWS = 1024


def _vmem_limit(nbytes):
    return int(min(V7X_VMEM_BYTES - 4 * 1024 * 1024, max(nbytes, 16 * 1024 * 1024)))


def _params(sem, vmem_bytes):
    return pltpu.CompilerParams(dimension_semantics=sem, vmem_limit_bytes=_vmem_limit(vmem_bytes))


def _dot(a, b):
    return jnp.dot(a, b, preferred_element_type=F32)


def _dot_nt(a, b):
    return lax.dot_general(a, b, (((1,), (1,)), ((), ())), preferred_element_type=F32)


def _dot_tn(a, b):
    return lax.dot_general(a, b, (((0,), (0,)), ((), ())), preferred_element_type=F32)


def _sigmoid(x):
    return 1.0 / (1.0 + jnp.exp(-x))


def _ln_kernel(x_ref, w_ref, b_ref, o_ref, obf_ref):
    x = x_ref[...]
    mu = jnp.mean(x, axis=-1, keepdims=True)
    xc = x - mu
    var = jnp.mean(xc * xc, axis=-1, keepdims=True)
    y = xc * lax.rsqrt(var + LN_EPS) * w_ref[...] + b_ref[...]
    o_ref[...] = y
    obf_ref[...] = y.astype(BF16)


def _layernorm(x, w, b):
    t, d = x.shape
    tr = min(LN_ROWS, t)
    return pl.pallas_call(
        _ln_kernel,
        grid=(t // tr,),
        in_specs=[pl.BlockSpec((tr, d), lambda i: (i, 0)),
                  pl.BlockSpec((1, d), lambda i: (0, 0)),
                  pl.BlockSpec((1, d), lambda i: (0, 0))],
        out_specs=[pl.BlockSpec((tr, d), lambda i: (i, 0)),
                   pl.BlockSpec((tr, d), lambda i: (i, 0))],
        out_shape=[jax.ShapeDtypeStruct((t, d), F32), jax.ShapeDtypeStruct((t, d), BF16)],
        compiler_params=_params(("parallel",), 2 * tr * d * 10 + (12 << 20)),
        name="layernorm",
    )(x, w.reshape(1, d), b.reshape(1, d))


def _mm_kernel(x_ref, w_ref, o_ref):
    o_ref[...] = _dot(x_ref[...], w_ref[...]).astype(o_ref.dtype)


def _matmul(x, w, layer, tm, tn, out_dtype, name):
    m, k = x.shape
    n = w.shape[2]
    tm, tn = min(tm, m), min(tn, n)
    vm = 2 * (tm * k * 2 + k * tn * 2 + tm * tn * 4) + tm * tn * 4 + (4 << 20)
    return pl.pallas_call(
        _mm_kernel,
        grid=(m // tm, n // tn),
        in_specs=[pl.BlockSpec((tm, k), lambda i, j: (i, 0)),
                  pl.BlockSpec((None, k, tn), lambda i, j: (layer, 0, j))],
        out_specs=pl.BlockSpec((tm, tn), lambda i, j: (i, j)),
        out_shape=jax.ShapeDtypeStruct((m, n), out_dtype),
        compiler_params=_params(("parallel", "arbitrary"), vm),
        name=name,
    )(x, w)


def _mm_res_kernel(x_ref, w_ref, r_ref, o_ref, *, alpha):
    o_ref[...] = alpha * r_ref[...] + _dot(x_ref[...], w_ref[...])


def _matmul_residual(x, w, layer, res, alpha, tm, tn, name):
    m, k = x.shape
    n = w.shape[2]
    tm, tn = min(tm, m), min(tn, n)
    vm = 2 * (tm * k * 2 + k * tn * 2 + 2 * tm * tn * 4) + tm * tn * 4 + (4 << 20)
    return pl.pallas_call(
        functools.partial(_mm_res_kernel, alpha=alpha),
        grid=(m // tm, n // tn),
        in_specs=[pl.BlockSpec((tm, k), lambda i, j: (i, 0)),
                  pl.BlockSpec((None, k, tn), lambda i, j: (layer, 0, j)),
                  pl.BlockSpec((tm, tn), lambda i, j: (i, j))],
        out_specs=pl.BlockSpec((tm, tn), lambda i, j: (i, j)),
        out_shape=jax.ShapeDtypeStruct((m, n), F32),
        compiler_params=_params(("parallel", "arbitrary"), vm),
        name=name,
    )(x, w, res)


def _hgrn_chunk(q, z, v, lb, st_ref, rev, row, pair_code):
    c_len = q.shape[0]
    n_lvl = c_len.bit_length() - 1
    t = jnp.exp(-jnp.abs(z))
    r = 1.0 / (1.0 + t)
    tr = t * r
    pos = z >= 0.0
    f = lb + (1.0 - lb) * jnp.where(pos, r, tr)
    kk = (1.0 - lb) * jnp.where(pos, tr, r)

    vb = v.astype(BF16)
    scores = _dot_nt(q.astype(BF16), kk.astype(BF16))
    ep = f
    ex = jnp.ones_like(f)
    et = f
    for lvl in range(n_lvl):
        m = 1 << lvl
        later = ((row & m) == 0) if rev else ((row & m) != 0)
        x = jnp.where(later, q * ep, kk * ex).astype(BF16)
        scores = jnp.where(pair_code >= m, _dot_nt(x, x), scores)
        from_earlier, from_later = (c_len - m, m) if rev else (m, c_len - m)
        sib = jnp.where(later, pltpu.roll(et, from_earlier, 0), pltpu.roll(et, from_later, 0))
        ep = jnp.where(later, ep * sib, ep)
        ex = jnp.where(later, ex, ex * sib)
        et = et * sib
    scores = jnp.where(pair_code < 0, 0.0, scores)

    st = st_ref[...]
    o = _dot_nt((q * ep).astype(BF16), st.astype(BF16)) + _dot(scores.astype(BF16), vb)
    st_ref[...] = et[0:1, :] * st + _dot_tn(vb, (kk * ex).astype(BF16))
    return o


def _hgrn_kernel(q_ref, zf_ref, zb_ref, v_ref, g_ref, lb_ref, nw_ref, o_ref,
                 of_scr, ob_scr, stf_scr, stb_scr, *, chunk):
    s_len = q_ref.shape[0]
    n_chunks = s_len // chunk
    row = lax.broadcasted_iota(jnp.int32, (chunk, HEAD_DIM), 0)
    ri = lax.broadcasted_iota(jnp.int32, (chunk, chunk), 0)
    ci = lax.broadcasted_iota(jnp.int32, (chunk, chunk), 1)
    code_f = jnp.where(ri >= ci, ri ^ ci, -1)
    code_b = jnp.where(ci >= ri, ri ^ ci, -1)
    n_heads = q_ref.shape[1] // HEAD_DIM
    stf_scr[...] = jnp.zeros_like(stf_scr)
    stb_scr[...] = jnp.zeros_like(stb_scr)

    def body(i, carry):
        rf = pl.multiple_of(i * chunk, chunk)
        rb = pl.multiple_of((n_chunks - 1 - i) * chunk, chunk)
        for hd in range(n_heads):
            sl = slice(hd * HEAD_DIM, (hd + 1) * HEAD_DIM)
            of_scr[pl.ds(rf, chunk), sl] = _hgrn_chunk(
                q_ref[pl.ds(rf, chunk), sl], zf_ref[pl.ds(rf, chunk), sl], v_ref[pl.ds(rf, chunk), sl],
                lb_ref[0:1, sl], stf_scr.at[hd], False, row, code_f)
            ob_scr[pl.ds(rb, chunk), sl] = _hgrn_chunk(
                q_ref[pl.ds(rb, chunk), sl], zb_ref[pl.ds(rb, chunk), sl], v_ref[pl.ds(rb, chunk), sl],
                lb_ref[1:2, sl], stb_scr.at[hd], True, row, code_b)
        return carry

    lax.fori_loop(0, n_chunks, body, 0)

    def epilogue(i, carry):
        r = pl.multiple_of(i * chunk, chunk)
        for hd in range(n_heads):
            sl = slice(hd * HEAD_DIM, (hd + 1) * HEAD_DIM)
            o = of_scr[pl.ds(r, chunk), sl] + ob_scr[pl.ds(r, chunk), sl]
            y = o * lax.rsqrt(jnp.mean(o * o, axis=-1, keepdims=True) + RMS_EPS) * nw_ref[:, sl]
            g = g_ref[pl.ds(r, chunk), sl]
            o_ref[pl.ds(r, chunk), sl] = (y * (g * _sigmoid(g))).astype(o_ref.dtype)
        return carry

    lax.fori_loop(0, n_chunks, epilogue, 0)


def _hgrn2(u3, lb, norm_w, a_w):
    bsz, s_len, _ = u3.shape
    heads = a_w // HEAD_DIM
    chunk = min(HG_CHUNK, s_len)
    hps = HG_HEADS_PER_STEP
    wid = hps * HEAD_DIM
    groups = heads // hps

    def col(k):
        return pl.BlockSpec((None, s_len, wid), lambda b, h: (b, 0, k * groups + h))

    out = pl.pallas_call(
        functools.partial(_hgrn_kernel, chunk=chunk),
        grid=(bsz, groups),
        in_specs=[col(0), col(1), col(2), col(3), col(4),
                  pl.BlockSpec((2, wid), lambda b, h: (0, h)),
                  pl.BlockSpec((1, wid), lambda b, h: (0, h))],
        out_specs=pl.BlockSpec((None, s_len, wid), lambda b, h: (b, 0, h)),
        out_shape=jax.ShapeDtypeStruct((bsz, s_len, a_w), BF16),
        scratch_shapes=[pltpu.VMEM((s_len, wid), F32), pltpu.VMEM((s_len, wid), F32),
                        pltpu.VMEM((hps, HEAD_DIM, HEAD_DIM), F32), pltpu.VMEM((hps, HEAD_DIM, HEAD_DIM), F32)],
        compiler_params=_params(("parallel", "arbitrary"), 14 * s_len * wid * 4 + (16 << 20)),
        name="hgrn2",
    )(u3, u3, u3, u3, u3, lb, norm_w.reshape(1, a_w))
    return out.reshape(bsz * s_len, a_w)


def _gelu_tanh(x):
    return 0.5 * x * (1.0 + jnp.tanh(math.sqrt(2.0 / math.pi) * (x + 0.044715 * (x * x * x))))


def _rglru_kernel(x_ref, xp_ref, xn_ref, cw_ref, cb_ref, w_ref, bias_ref, lam_ref, *rest, rev):
    if rev:
        gate_ref, hf_ref, o_ref, carry_scr = rest
    else:
        o_ref, carry_scr = rest
    ts, width = x_ref.shape
    n_grp = ts // SUBLANES
    step = pl.program_id(1)
    tile = pl.num_programs(1) - 1 - step if rev else step

    @pl.when(step == 0)
    def _():
        carry_scr[...] = jnp.zeros_like(carry_scr)

    has_prev = (tile > 0).astype(F32)
    has_next = (tile < pl.num_programs(1) - 1).astype(F32)
    x = x_ref[...]
    row = lax.broadcasted_iota(jnp.int32, (ts, width), 0)
    p6 = xp_ref[SUBLANES - 2:SUBLANES - 1, :] * has_prev
    p7 = xp_ref[SUBLANES - 1:SUBLANES, :] * has_prev
    n0 = xn_ref[0:1, :] * has_next
    xm1 = jnp.where(row == 0, p7, pltpu.roll(x, 1, 0))
    xm2 = jnp.where(row == 0, p6, jnp.where(row == 1, p7, pltpu.roll(x, 2, 0)))
    xp1 = jnp.where(row == ts - 1, n0, pltpu.roll(x, ts - 1, 0))
    xc = (cb_ref[...] + cw_ref[0:1, :] * xm2 + cw_ref[1:2, :] * xm1
          + cw_ref[2:3, :] * x + cw_ref[3:4, :] * xp1)
    lam = lam_ref[...]
    sp = jnp.maximum(-lam, 0.0) + jnp.log1p(jnp.exp(-jnp.abs(lam)))
    sub = lax.broadcasted_iota(jnp.int32, (n_grp, SUBLANES, HEAD_DIM), 1)
    for n in range(width // HEAD_DIM):
        sl = slice(n * HEAD_DIM, (n + 1) * HEAD_DIM)
        xn_blk = xc[:, sl]
        zz = _dot(xn_blk.astype(BF16), w_ref[n])
        r = _sigmoid(zz[:, :HEAD_DIM] + bias_ref[0:1, sl])
        ig = _sigmoid(zz[:, HEAD_DIM:] + bias_ref[1:2, sl])
        log_a = (-RG_C) * r * sp[:, sl]
        a = jnp.exp(log_a)
        b = jnp.sqrt(-jnp.tanh(log_a) * (a * a + 1.0)) * (ig * xn_blk)
        a = a.reshape(n_grp, SUBLANES, HEAD_DIM)
        b = b.reshape(n_grp, SUBLANES, HEAD_DIM)
        for d in (1, 2, 4):
            if rev:
                a_s, b_s, ok = pltpu.roll(a, SUBLANES - d, 1), pltpu.roll(b, SUBLANES - d, 1), sub < SUBLANES - d
            else:
                a_s, b_s, ok = pltpu.roll(a, d, 1), pltpu.roll(b, d, 1), sub >= d
            b = jnp.where(ok, a * b_s + b, b)
            a = jnp.where(ok, a * a_s, a)
        h_in = jnp.broadcast_to(carry_scr[:, sl], (SUBLANES, HEAD_DIM))
        last = 0 if rev else SUBLANES - 1
        for g in (range(n_grp - 1, -1, -1) if rev else range(n_grp)):
            rows = slice(g * SUBLANES, (g + 1) * SUBLANES)
            h = a[g] * h_in + b[g]
            h_in = jnp.broadcast_to(h[last:last + 1, :], (SUBLANES, HEAD_DIM))
            if rev:
                o_ref[rows, sl] = (_gelu_tanh(gate_ref[rows, sl]) * (hf_ref[rows, sl] + h)).astype(o_ref.dtype)
            else:
                o_ref[rows, sl] = h
        carry_scr[:, sl] = h_in[0:1, :]


def _rglru_dir(u3, col0, conv_w, conv_b, w_dir, bias_dir, lam_dir, b_w, hf=None):
    bsz, s_len, _ = u3.shape
    rev = hf is not None
    ts = min(RG_ROWS, s_len)
    nt = s_len // ts
    cb = col0 // b_w
    rows8 = ts // SUBLANES

    def tile(i):
        return nt - 1 - i if rev else i

    in_specs = [pl.BlockSpec((None, ts, b_w), lambda b, i: (b, tile(i), cb)),
                pl.BlockSpec((None, SUBLANES, b_w), lambda b, i: (b, jnp.maximum(tile(i) * rows8 - 1, 0), cb)),
                pl.BlockSpec((None, SUBLANES, b_w),
                             lambda b, i: (b, jnp.minimum((tile(i) + 1) * rows8, s_len // SUBLANES - 1), cb)),
                pl.BlockSpec((4, b_w), lambda b, i: (0, 0)),
                pl.BlockSpec((1, b_w), lambda b, i: (0, 0)),
                pl.BlockSpec(w_dir.shape, lambda b, i: (0, 0, 0)),
                pl.BlockSpec((2, b_w), lambda b, i: (0, 0)),
                pl.BlockSpec((1, b_w), lambda b, i: (0, 0))]
    args = [u3, u3, u3, conv_w, conv_b.reshape(1, b_w), w_dir, bias_dir, lam_dir.reshape(1, b_w)]
    if rev:
        in_specs += [pl.BlockSpec((None, ts, b_w), lambda b, i: (b, tile(i), cb + 1)),
                     pl.BlockSpec((None, ts, b_w), lambda b, i: (b, tile(i), 0))]
        args += [u3, hf]
    return pl.pallas_call(
        functools.partial(_rglru_kernel, rev=rev),
        grid=(bsz, nt),
        in_specs=in_specs,
        out_specs=pl.BlockSpec((None, ts, b_w), lambda b, i: (b, tile(i), 0)),
        out_shape=jax.ShapeDtypeStruct((bsz, s_len, b_w), BF16 if rev else F32),
        scratch_shapes=[pltpu.VMEM((1, b_w), F32)],
        compiler_params=_params(("parallel", "arbitrary"), 2 * 4 * ts * b_w * 4 + (24 << 20)),
        name="rglru_bwd" if rev else "rglru_fwd",
    )(*args)


def _attn_prep_kernel(q0_ref, q1_ref, kv_ref, cos_ref, sa_ref, sb_ref, qn_ref, kn_ref,
                      qo_ref, ko_ref, vo_ref):
    cos = cos_ref[...]
    sin_a = sa_ref[...]
    sin_b = sb_ref[...]

    def norm_rope(t, w):
        y = t * lax.rsqrt(jnp.mean(t * t, axis=-1, keepdims=True) + RMS_EPS) * w
        return (y * cos + pltpu.roll(y, HEAD_DIM - ROPE_HALF // 2, 1) * sin_a
                + pltpu.roll(y, ROPE_HALF // 2, 1) * sin_b)

    half_heads = q0_ref.shape[1] // HEAD_DIM
    kv_heads = ko_ref.shape[1] // HEAD_DIM
    qn = qn_ref[...]
    kn = kn_ref[...]
    for hd in range(half_heads):
        sl = slice(hd * HEAD_DIM, (hd + 1) * HEAD_DIM)
        so = slice((half_heads + hd) * HEAD_DIM, (half_heads + hd + 1) * HEAD_DIM)
        qo_ref[:, sl] = norm_rope(q0_ref[:, sl], qn).astype(BF16)
        qo_ref[:, so] = norm_rope(q1_ref[:, sl], qn).astype(BF16)
    for hd in range(kv_heads):
        sl = slice(hd * HEAD_DIM, (hd + 1) * HEAD_DIM)
        ko_ref[:, sl] = norm_rope(kv_ref[:, sl], kn).astype(BF16)
    vo_ref[...] = kv_ref[:, kv_heads * HEAD_DIM:].T.astype(BF16)


def _attn_prep(u2, col0, c_w, kv_w, s_len, cos, sin_a, sin_b, qn_w, kn_w):
    t = u2.shape[0]
    tr = min(PREP_ROWS, s_len)
    half = c_w // 2
    assert col0 % half == 0 and 2 * kv_w == half
    cb = col0 // half
    npos = s_len // tr
    tab = pl.BlockSpec((tr, HEAD_DIM), lambda i: (i % npos, 0))
    vec = pl.BlockSpec((1, HEAD_DIM), lambda i: (0, 0))
    return pl.pallas_call(
        _attn_prep_kernel,
        grid=(t // tr,),
        in_specs=[pl.BlockSpec((tr, half), lambda i: (i, cb)),
                  pl.BlockSpec((tr, half), lambda i: (i, cb + 1)),
                  pl.BlockSpec((tr, half), lambda i: (i, cb + 2)),
                  tab, tab, tab, vec, vec],
        out_specs=[pl.BlockSpec((tr, c_w), lambda i: (i, 0)),
                   pl.BlockSpec((tr, kv_w), lambda i: (i, 0)),
                   pl.BlockSpec((kv_w, tr), lambda i: (0, i))],
        out_shape=[jax.ShapeDtypeStruct((t, c_w), BF16),
                   jax.ShapeDtypeStruct((t, kv_w), BF16),
                   jax.ShapeDtypeStruct((kv_w, t), BF16)],
        compiler_params=_params(("parallel",), 2 * tr * (3 * half * 4 + (c_w + 2 * kv_w) * 2) + (16 << 20)),
        name="attn_prep",
    )(u2, u2, u2, cos, sin_a, sin_b, qn_w.reshape(1, HEAD_DIM), kn_w.reshape(1, HEAD_DIM))


def _attn_kernel(q_ref, k_ref, vt_ref, o_ref, sc_scr):
    k = k_ref[...]
    vt = vt_ref[...]
    vt_aug = jnp.concatenate([vt, jnp.ones((BF16_ROWS, vt.shape[1]), BF16)], axis=0)
    s_len = k.shape[0]
    kc = min(ATTN_KB, s_len)
    n_kc = s_len // kc

    def scores(g, c, run_max):
        rows = slice(c * kc, (c + 1) * kc)
        sc = _dot_nt(k[rows, :], q_ref[:, g * HEAD_DIM:(g + 1) * HEAD_DIM])
        sc_scr[g % 2, rows, :] = sc
        cmax = jnp.max(sc.reshape(kc // SUBLANES, SUBLANES, sc.shape[1]), axis=0)
        return cmax if run_max is None else jnp.maximum(run_max, cmax)

    next_max = None
    for c in range(n_kc):
        next_max = scores(0, c, next_max)
    for g in range(C_GROUP):
        mx = jnp.max(next_max, axis=0, keepdims=True)
        next_max = None
        acc = None
        for c in range(n_kc):
            rows = slice(c * kc, (c + 1) * kc)
            p = jnp.exp2((sc_scr[g % 2, rows, :] - mx) * (ATTN_SCALE * LOG2_E)).astype(BF16)
            if g + 1 < C_GROUP:
                next_max = scores(g + 1, c, next_max)
            part = _dot(vt_aug[:, rows], p)
            acc = part if acc is None else acc + part
        ot = acc[:HEAD_DIM, :] / acc[HEAD_DIM:HEAD_DIM + 1, :]
        o_ref[:, g * HEAD_DIM:(g + 1) * HEAD_DIM] = ot.T.astype(o_ref.dtype)


def _attention(qr, kr, vt, bsz, s_len):
    t, c_w = qr.shape
    kv_heads = kr.shape[1] // HEAD_DIM
    gw = C_GROUP * HEAD_DIM
    tq = min(ATTN_TQ, s_len)
    nq = s_len // tq
    return pl.pallas_call(
        _attn_kernel,
        grid=(bsz, kv_heads, nq),
        in_specs=[pl.BlockSpec((tq, gw), lambda b, h, i: (b * nq + i, h)),
                  pl.BlockSpec((s_len, HEAD_DIM), lambda b, h, i: (b, h)),
                  pl.BlockSpec((HEAD_DIM, s_len), lambda b, h, i: (h, b))],
        out_specs=pl.BlockSpec((tq, gw), lambda b, h, i: (b * nq + i, h)),
        out_shape=jax.ShapeDtypeStruct((t, c_w), BF16),
        scratch_shapes=[pltpu.VMEM((2, s_len, tq), F32)],
        compiler_params=_params(("parallel", "parallel", "arbitrary"),
                                8 * tq * s_len * 4 + 4 * s_len * HEAD_DIM * 2 + (8 << 20)),
        name="attention",
    )(qr, kr, vt)


def _outproj_kernel(ya_ref, yb_ref, yc_ref, wa_ref, wb_ref, wc_ref, res_ref, o_ref, *, alpha):
    acc = _dot(ya_ref[...], wa_ref[...])
    acc = acc + _dot(yb_ref[...], wb_ref[...])
    acc = acc + _dot(yc_ref[...], wc_ref[...])
    o_ref[...] = alpha * res_ref[...] + acc


def _outproj(ya, yb, yc, w_out, layer, res, alpha):
    t, a_w = ya.shape
    b_w = yb.shape[1]
    c_w = yc.shape[1]
    n = w_out.shape[2]
    assert a_w == b_w and (a_w + b_w) % c_w == 0
    tm = min(OUT_TM, t)
    tn = min(MM_TN, n)
    c_blk = (a_w + b_w) // c_w
    vm = 2 * (tm * (a_w + b_w + c_w) * 2 + (a_w + b_w + c_w) * tn * 2 + 2 * tm * tn * 4) + tm * tn * 4 + (4 << 20)
    return pl.pallas_call(
        functools.partial(_outproj_kernel, alpha=alpha),
        grid=(t // tm, n // tn),
        in_specs=[pl.BlockSpec((tm, a_w), lambda i, j: (i, 0)),
                  pl.BlockSpec((tm, b_w), lambda i, j: (i, 0)),
                  pl.BlockSpec((tm, c_w), lambda i, j: (i, 0)),
                  pl.BlockSpec((None, a_w, tn), lambda i, j: (layer, 0, j)),
                  pl.BlockSpec((None, b_w, tn), lambda i, j: (layer, 1, j)),
                  pl.BlockSpec((None, c_w, tn), lambda i, j: (layer, c_blk, j)),
                  pl.BlockSpec((tm, tn), lambda i, j: (i, j))],
        out_specs=pl.BlockSpec((tm, tn), lambda i, j: (i, j)),
        out_shape=jax.ShapeDtypeStruct((t, n), F32),
        compiler_params=_params(("parallel", "arbitrary"), vm),
        name="outproj",
    )(ya, yb, yc, w_out, w_out, w_out, res)


def _ffn_up_kernel(x_ref, ex_ref, wg_ref, wu_ref, cw_ref, cb_ref, o_ref, wg_scr, wu_scr, edge_scr,
                   *, tiles_per_seq):
    tm, tn = o_ref.shape
    i = pl.program_id(1)

    @pl.when(i == 0)
    def _():
        wg_scr[...] = wg_ref[...].astype(BF16)
        wu_scr[...] = wu_ref[...].astype(BF16)
        edge_scr[...] = _dot(ex_ref[...], wg_scr[...])

    x = x_ref[...]
    g = _dot(x, wg_scr[...])
    up = _dot(x, wu_scr[...])
    pos = i % tiles_per_seq
    has_prev = (pos > 0).astype(F32)
    has_next = (pos < tiles_per_seq - 1).astype(F32)
    last_edge = edge_scr.shape[0] - 1
    g_prev = edge_scr[pl.ds(jnp.maximum(2 * i - 1, 0), 1), :] * has_prev
    g_next = edge_scr[pl.ds(jnp.minimum(2 * i + 2, last_edge), 1), :] * has_next
    row = lax.broadcasted_iota(jnp.int32, (tm, tn), 0)
    g_m1 = jnp.where(row == 0, g_prev, pltpu.roll(g, 1, 0))
    g_p1 = jnp.where(row == tm - 1, g_next, pltpu.roll(g, tm - 1, 0))
    y = cb_ref[...] + cw_ref[0:1, :] * g_m1 + cw_ref[1:2, :] * g + cw_ref[2:3, :] * g_p1
    o_ref[...] = ((y * _sigmoid(y)) * up).astype(o_ref.dtype)


def _ffn_up(hbf, w_up, layer, conv_w, conv_b, s_len):
    t, d = hbf.shape
    d_ff = w_up.shape[2] // 2
    tm = min(FFN_TM, s_len)
    tn = FFN_TN
    assert d_ff % tn == 0
    nj = d_ff // tn
    n_tiles = t // tm
    tiles_per_seq = s_len // tm
    h3 = hbf.reshape(n_tiles, tm, d)
    edge_x = jnp.stack([h3[:, 0, :], h3[:, tm - 1, :]], axis=1).reshape(2 * n_tiles, d)
    vm = 2 * (tm * d * 2 + 2 * d * tn * 4 + tm * tn * 2) + 2 * d * tn * 2 + 8 * tm * tn * 4 + (4 << 20)
    w_scr = pltpu.VMEM((d, tn), BF16)
    return pl.pallas_call(
        functools.partial(_ffn_up_kernel, tiles_per_seq=tiles_per_seq),
        grid=(nj, n_tiles),
        in_specs=[pl.BlockSpec((tm, d), lambda j, i: (i, 0)),
                  pl.BlockSpec((2 * n_tiles, d), lambda j, i: (0, 0)),
                  pl.BlockSpec((None, d, tn), lambda j, i: (layer, 0, j)),
                  pl.BlockSpec((None, d, tn), lambda j, i: (layer, 0, nj + j)),
                  pl.BlockSpec((3, tn), lambda j, i: (0, j)),
                  pl.BlockSpec((1, tn), lambda j, i: (0, j))],
        out_specs=pl.BlockSpec((tm, tn), lambda j, i: (i, j)),
        out_shape=jax.ShapeDtypeStruct((t, d_ff), BF16),
        scratch_shapes=[w_scr, w_scr, pltpu.VMEM((2 * n_tiles, tn), F32)],
        compiler_params=_params(("parallel", "arbitrary"), vm),
        name="ffn_up",
    )(hbf, edge_x, w_up, w_up, conv_w, conv_b.reshape(1, d_ff))


def _rope_tables(s_len):
    rows = s_len // GRID_W
    g_r, g_c = jnp.meshgrid(jnp.arange(rows), jnp.arange(GRID_W), indexing='ij')
    row = g_r.reshape(s_len).astype(F32)
    colp = g_c.reshape(s_len).astype(F32)
    inv_freq = ROPE_THETA ** (-jnp.arange(0, ROPE_HALF, 2, dtype=F32) / ROPE_HALF)
    ang_r = row[:, None] * inv_freq[None, :]
    ang_c = colp[:, None] * inv_freq[None, :]
    ang = jnp.concatenate([ang_r, ang_r, ang_c, ang_c], axis=-1)
    cos, sin = jnp.cos(ang), jnp.sin(ang)
    first_quarter = (jnp.arange(HEAD_DIM) & (ROPE_HALF // 2)) == 0
    sin_a = jnp.where(first_quarter, -sin, 0.0)
    sin_b = jnp.where(first_quarter, 0.0, sin)
    return cos, sin_a, sin_b


def kernel(x, emb_ln_w, emb_ln_b, w_in, hgrn_lb_logits, hgrn_norm_w, rglru_conv_w, rglru_conv_b,
           rglru_wa, rglru_ba, rglru_wx, rglru_bx, rglru_lambda, attn_q_norm_w, attn_k_norm_w,
           w_out, ln1_w, ln1_b, ffn_w_up, ffn_conv_w, ffn_conv_b, ffn_w_down, ln2_w, ln2_b):
    bsz, s_len, d_model = x.shape
    depth = w_in.shape[0]
    t = bsz * s_len
    a_w = hgrn_norm_w.shape[1]
    b_w = rglru_conv_w.shape[2]
    in_cols = w_in.shape[2]
    a_cols = 5 * a_w
    b_cols = 2 * b_w
    c_w = w_out.shape[1] - a_w - b_w
    kv_w = (in_cols - a_cols - b_cols - c_w) // 2
    alpha = (2.0 * depth) ** 0.25

    cos, sin_a, sin_b = _rope_tables(s_len)
    lb_cs = jnp.cumsum(jax.nn.softmax(hgrn_lb_logits.astype(F32), axis=0), axis=0)
    lower_bounds = lb_cs - lb_cs[0:1]

    w_in_bf = w_in.astype(BF16)
    w_out_bf = w_out.astype(BF16)
    w_down_bf = ffn_w_down.astype(BF16)

    h, hbf = _layernorm(x.reshape(t, d_model), emb_ln_w, emb_ln_b)
    for l in range(depth):
        u2 = _matmul(hbf, w_in_bf, l, MM_TM, MM_TN, F32, "in_proj")
        u3 = u2.reshape(bsz, s_len, in_cols)

        ya = _hgrn2(u3, lower_bounds[l], hgrn_norm_w[l], a_w)

        yb = None
        for d in range(2):
            w_dir = jnp.concatenate([rglru_wa[l, d], rglru_wx[l, d]], axis=-1).astype(BF16)
            bias_dir = jnp.stack([rglru_ba[l, d], rglru_bx[l, d]])
            yb = _rglru_dir(u3, a_cols, rglru_conv_w[l], rglru_conv_b[l], w_dir, bias_dir,
                            rglru_lambda[l, d], b_w, hf=yb)
        yb = yb.reshape(t, b_w)

        qr, kr, vt = _attn_prep(u2, a_cols + b_cols, c_w, kv_w, s_len, cos, sin_a, sin_b,
                                attn_q_norm_w[l], attn_k_norm_w[l])
        yc = _attention(qr, kr, vt, bsz, s_len)

        pre = _outproj(ya, yb, yc, w_out_bf, l, h, alpha)
        h, hbf = _layernorm(pre, ln1_w[l], ln1_b[l])

        act = _ffn_up(hbf, ffn_w_up, l, ffn_conv_w[l], ffn_conv_b[l], s_len)
        pre = _matmul_residual(act, w_down_bf, l, h, alpha, DOWN_TM, DOWN_TN, "ffn_down")
        h, hbf = _layernorm(pre, ln2_w[l], ln2_b[l])
    return h.reshape(bsz, s_len, d_model)
```

```python
import functools
import math

import jax
import jax.numpy as jnp
from jax import lax
from jax.experimental import pallas as pl
from jax.experimental.pallas import tpu as pltpu

F32 = jnp.float32
BF16 = jnp.bfloat16

HEAD_DIM = 128
RG_C = 8.0
C_GROUP = 4
ROPE_THETA = 10000.0
ROPE_HALF = HEAD_DIM // 2
GRID_W = 64
LN_EPS = 1e-5
RMS_EPS = 1e-6
ATTN_SCALE = HEAD_DIM ** -0.5
LOG2_E = math.log2(math.e)

V7X_VMEM_BYTES = 64 * 1024 * 1024
SUBLANES = 8
BF16_ROWS = 16

HG_CHUNK = 256
HG_HEADS_PER_STEP = 4
LN_ROWS = 512
MM_TM = 1024
MM_TN = 1024
OUT_TM = 1024
FFN_TM = 1024
FFN_TN = 256
DOWN_TM = 512
DOWN_TN = 512
ATTN_TQ = 2048
ATTN_KB = 1024
PREP_ROWS = 1024
RG_ROWS = 1024


def _vmem_limit(nbytes):
    return int(min(V7X_VMEM_BYTES - 4 * 1024 * 1024, max(nbytes, 16 * 1024 * 1024)))


def _params(sem, vmem_bytes):
    return pltpu.CompilerParams(dimension_semantics=sem, vmem_limit_bytes=_vmem_limit(vmem_bytes))


def _dot(a, b):
    return jnp.dot(a, b, preferred_element_type=F32)


def _dot_nt(a, b):
    return lax.dot_general(a, b, (((1,), (1,)), ((), ())), preferred_element_type=F32)


def _dot_tn(a, b):
    return lax.dot_general(a, b, (((0,), (0,)), ((), ())), preferred_element_type=F32)


def _sigmoid(x):
    return 1.0 / (1.0 + jnp.exp(-x))


def _ln_kernel(x_ref, w_ref, b_ref, o_ref, obf_ref):
    x = x_ref[...]
    mu = jnp.mean(x, axis=-1, keepdims=True)
    xc = x - mu
    var = jnp.mean(xc * xc, axis=-1, keepdims=True)
    y = xc * lax.rsqrt(var + LN_EPS) * w_ref[...] + b_ref[...]
    o_ref[...] = y
    obf_ref[...] = y.astype(BF16)


def _layernorm(x, w, b):
    t, d = x.shape
    tr = min(LN_ROWS, t)
    return pl.pallas_call(
        _ln_kernel,
        grid=(t // tr,),
        in_specs=[pl.BlockSpec((tr, d), lambda i: (i, 0)),
                  pl.BlockSpec((1, d), lambda i: (0, 0)),
                  pl.BlockSpec((1, d), lambda i: (0, 0))],
        out_specs=[pl.BlockSpec((tr, d), lambda i: (i, 0)),
                   pl.BlockSpec((tr, d), lambda i: (i, 0))],
        out_shape=[jax.ShapeDtypeStruct((t, d), F32), jax.ShapeDtypeStruct((t, d), BF16)],
        compiler_params=_params(("parallel",), 2 * tr * d * 10 + (12 << 20)),
        name="layernorm",
    )(x, w.reshape(1, d), b.reshape(1, d))


def _mm_kernel(x_ref, w_ref, o_ref):
    o_ref[...] = _dot(x_ref[...], w_ref[...]).astype(o_ref.dtype)


def _matmul(x, w, layer, tm, tn, out_dtype, name):
    m, k = x.shape
    n = w.shape[2]
    tm, tn = min(tm, m), min(tn, n)
    vm = 2 * (tm * k * 2 + k * tn * 2 + tm * tn * 4) + tm * tn * 4 + (4 << 20)
    return pl.pallas_call(
        _mm_kernel,
        grid=(m // tm, n // tn),
        in_specs=[pl.BlockSpec((tm, k), lambda i, j: (i, 0)),
                  pl.BlockSpec((None, k, tn), lambda i, j: (layer, 0, j))],
        out_specs=pl.BlockSpec((tm, tn), lambda i, j: (i, j)),
        out_shape=jax.ShapeDtypeStruct((m, n), out_dtype),
        compiler_params=_params(("parallel", "arbitrary"), vm),
        name=name,
    )(x, w)


def _mm_res_kernel(x_ref, w_ref, r_ref, o_ref, *, alpha):
    o_ref[...] = alpha * r_ref[...] + _dot(x_ref[...], w_ref[...])


def _matmul_residual(x, w, layer, res, alpha, tm, tn, name):
    m, k = x.shape
    n = w.shape[2]
    tm, tn = min(tm, m), min(tn, n)
    vm = 2 * (tm * k * 2 + k * tn * 2 + 2 * tm * tn * 4) + tm * tn * 4 + (4 << 20)
    return pl.pallas_call(
        functools.partial(_mm_res_kernel, alpha=alpha),
        grid=(m // tm, n // tn),
        in_specs=[pl.BlockSpec((tm, k), lambda i, j: (i, 0)),
                  pl.BlockSpec((None, k, tn), lambda i, j: (layer, 0, j)),
                  pl.BlockSpec((tm, tn), lambda i, j: (i, j))],
        out_specs=pl.BlockSpec((tm, tn), lambda i, j: (i, j)),
        out_shape=jax.ShapeDtypeStruct((m, n), F32),
        compiler_params=_params(("parallel", "arbitrary"), vm),
        name=name,
    )(x, w, res)


def _hgrn_chunk(q, z, v, lb, st_ref, rev, row, pair_code):
    c_len = q.shape[0]
    n_lvl = c_len.bit_length() - 1
    t = jnp.exp(-jnp.abs(z))
    r = 1.0 / (1.0 + t)
    tr = t * r
    pos = z >= 0.0
    f = lb + (1.0 - lb) * jnp.where(pos, r, tr)
    kk = (1.0 - lb) * jnp.where(pos, tr, r)

    vb = v.astype(BF16)
    scores = _dot_nt(q.astype(BF16), kk.astype(BF16))
    ep = f
    ex = jnp.ones_like(f)
    et = f
    for lvl in range(n_lvl):
        m = 1 << lvl
        later = ((row & m) == 0) if rev else ((row & m) != 0)
        x = jnp.where(later, q * ep, kk * ex).astype(BF16)
        scores = jnp.where(pair_code >= m, _dot_nt(x, x), scores)
        from_earlier, from_later = (c_len - m, m) if rev else (m, c_len - m)
        sib = jnp.where(later, pltpu.roll(et, from_earlier, 0), pltpu.roll(et, from_later, 0))
        ep = jnp.where(later, ep * sib, ep)
        ex = jnp.where(later, ex, ex * sib)
        et = et * sib
    scores = jnp.where(pair_code < 0, 0.0, scores)

    st = st_ref[...]
    o = _dot_nt((q * ep).astype(BF16), st.astype(BF16)) + _dot(scores.astype(BF16), vb)
    st_ref[...] = et[0:1, :] * st + _dot_tn(vb, (kk * ex).astype(BF16))
    return o


def _hgrn_kernel(q_ref, zf_ref, zb_ref, v_ref, g_ref, lb_ref, nw_ref, o_ref,
                 of_scr, ob_scr, stf_scr, stb_scr, *, chunk):
    s_len = q_ref.shape[0]
    n_chunks = s_len // chunk
    row = lax.broadcasted_iota(jnp.int32, (chunk, HEAD_DIM), 0)
    ri = lax.broadcasted_iota(jnp.int32, (chunk, chunk), 0)
    ci = lax.broadcasted_iota(jnp.int32, (chunk, chunk), 1)
    code_f = jnp.where(ri >= ci, ri ^ ci, -1)
    code_b = jnp.where(ci >= ri, ri ^ ci, -1)
    n_heads = q_ref.shape[1] // HEAD_DIM
    stf_scr[...] = jnp.zeros_like(stf_scr)
    stb_scr[...] = jnp.zeros_like(stb_scr)

    def body(i, carry):
        rf = pl.multiple_of(i * chunk, chunk)
        rb = pl.multiple_of((n_chunks - 1 - i) * chunk, chunk)
        for hd in range(n_heads):
            sl = slice(hd * HEAD_DIM, (hd + 1) * HEAD_DIM)
            of_scr[pl.ds(rf, chunk), sl] = _hgrn_chunk(
                q_ref[pl.ds(rf, chunk), sl], zf_ref[pl.ds(rf, chunk), sl], v_ref[pl.ds(rf, chunk), sl],
                lb_ref[0:1, sl], stf_scr.at[hd], False, row, code_f)
            ob_scr[pl.ds(rb, chunk), sl] = _hgrn_chunk(
                q_ref[pl.ds(rb, chunk), sl], zb_ref[pl.ds(rb, chunk), sl], v_ref[pl.ds(rb, chunk), sl],
                lb_ref[1:2, sl], stb_scr.at[hd], True, row, code_b)
        return carry

    lax.fori_loop(0, n_chunks, body, 0)

    def epilogue(i, carry):
        r = pl.multiple_of(i * chunk, chunk)
        for hd in range(n_heads):
            sl = slice(hd * HEAD_DIM, (hd + 1) * HEAD_DIM)
            o = of_scr[pl.ds(r, chunk), sl] + ob_scr[pl.ds(r, chunk), sl]
            y = o * lax.rsqrt(jnp.mean(o * o, axis=-1, keepdims=True) + RMS_EPS) * nw_ref[:, sl]
            g = g_ref[pl.ds(r, chunk), sl]
            o_ref[pl.ds(r, chunk), sl] = (y * (g * _sigmoid(g))).astype(o_ref.dtype)
        return carry

    lax.fori_loop(0, n_chunks, epilogue, 0)


def _hgrn2(u3, lb, norm_w, a_w):
    bsz, s_len, _ = u3.shape
    heads = a_w // HEAD_DIM
    chunk = min(HG_CHUNK, s_len)
    hps = HG_HEADS_PER_STEP
    wid = hps * HEAD_DIM
    groups = heads // hps

    def col(k):
        return pl.BlockSpec((None, s_len, wid), lambda b, h: (b, 0, k * groups + h))

    out = pl.pallas_call(
        functools.partial(_hgrn_kernel, chunk=chunk),
        grid=(bsz, groups),
        in_specs=[col(0), col(1), col(2), col(3), col(4),
                  pl.BlockSpec((2, wid), lambda b, h: (0, h)),
                  pl.BlockSpec((1, wid), lambda b, h: (0, h))],
        out_specs=pl.BlockSpec((None, s_len, wid), lambda b, h: (b, 0, h)),
        out_shape=jax.ShapeDtypeStruct((bsz, s_len, a_w), BF16),
        scratch_shapes=[pltpu.VMEM((s_len, wid), F32), pltpu.VMEM((s_len, wid), F32),
                        pltpu.VMEM((hps, HEAD_DIM, HEAD_DIM), F32), pltpu.VMEM((hps, HEAD_DIM, HEAD_DIM), F32)],
        compiler_params=_params(("parallel", "arbitrary"), 14 * s_len * wid * 4 + (16 << 20)),
        name="hgrn2",
    )(u3, u3, u3, u3, u3, lb, norm_w.reshape(1, a_w))
    return out.reshape(bsz * s_len, a_w)


def _gelu_tanh(x):
    return 0.5 * x * (1.0 + jnp.tanh(math.sqrt(2.0 / math.pi) * (x + 0.044715 * (x * x * x))))


def _rglru_kernel(x_ref, xp_ref, xn_ref, cw_ref, cb_ref, w_ref, bias_ref, lam_ref, *rest, rev):
    if rev:
        gate_ref, hf_ref, o_ref, carry_scr = rest
    else:
        o_ref, carry_scr = rest
    ts, width = x_ref.shape
    n_grp = ts // SUBLANES
    step = pl.program_id(1)
    tile = pl.num_programs(1) - 1 - step if rev else step

    @pl.when(step == 0)
    def _():
        carry_scr[...] = jnp.zeros_like(carry_scr)

    has_prev = (tile > 0).astype(F32)
    has_next = (tile < pl.num_programs(1) - 1).astype(F32)
    x = x_ref[...]
    row = lax.broadcasted_iota(jnp.int32, (ts, width), 0)
    p6 = xp_ref[SUBLANES - 2:SUBLANES - 1, :] * has_prev
    p7 = xp_ref[SUBLANES - 1:SUBLANES, :] * has_prev
    n0 = xn_ref[0:1, :] * has_next
    xm1 = jnp.where(row == 0, p7, pltpu.roll(x, 1, 0))
    xm2 = jnp.where(row == 0, p6, jnp.where(row == 1, p7, pltpu.roll(x, 2, 0)))
    xp1 = jnp.where(row == ts - 1, n0, pltpu.roll(x, ts - 1, 0))
    xc = (cb_ref[...] + cw_ref[0:1, :] * xm2 + cw_ref[1:2, :] * xm1
          + cw_ref[2:3, :] * x + cw_ref[3:4, :] * xp1)
    lam = lam_ref[...]
    sp = jnp.maximum(-lam, 0.0) + jnp.log1p(jnp.exp(-jnp.abs(lam)))
    sub = lax.broadcasted_iota(jnp.int32, (n_grp, SUBLANES, HEAD_DIM), 1)
    for n in range(width // HEAD_DIM):
        sl = slice(n * HEAD_DIM, (n + 1) * HEAD_DIM)
        xn_blk = xc[:, sl]
        zz = _dot(xn_blk.astype(BF16), w_ref[n])
        r = _sigmoid(zz[:, :HEAD_DIM] + bias_ref[0:1, sl])
        ig = _sigmoid(zz[:, HEAD_DIM:] + bias_ref[1:2, sl])
        log_a = (-RG_C) * r * sp[:, sl]
        a = jnp.exp(log_a)
        b = jnp.sqrt(-jnp.tanh(log_a) * (a * a + 1.0)) * (ig * xn_blk)
        a = a.reshape(n_grp, SUBLANES, HEAD_DIM)
        b = b.reshape(n_grp, SUBLANES, HEAD_DIM)
        for d in (1, 2, 4):
            if rev:
                a_s, b_s, ok = pltpu.roll(a, SUBLANES - d, 1), pltpu.roll(b, SUBLANES - d, 1), sub < SUBLANES - d
            else:
                a_s, b_s, ok = pltpu.roll(a, d, 1), pltpu.roll(b, d, 1), sub >= d
            b = jnp.where(ok, a * b_s + b, b)
            a = jnp.where(ok, a * a_s, a)
        h_in = jnp.broadcast_to(carry_scr[:, sl], (SUBLANES, HEAD_DIM))
        last = 0 if rev else SUBLANES - 1
        for g in (range(n_grp - 1, -1, -1) if rev else range(n_grp)):
            rows = slice(g * SUBLANES, (g + 1) * SUBLANES)
            h = a[g] * h_in + b[g]
            h_in = jnp.broadcast_to(h[last:last + 1, :], (SUBLANES, HEAD_DIM))
            if rev:
                o_ref[rows, sl] = (_gelu_tanh(gate_ref[rows, sl]) * (hf_ref[rows, sl] + h)).astype(o_ref.dtype)
            else:
                o_ref[rows, sl] = h
        carry_scr[:, sl] = h_in[0:1, :]


def _rglru_dir(u3, col0, conv_w, conv_b, w_dir, bias_dir, lam_dir, b_w, hf=None):
    bsz, s_len, _ = u3.shape
    rev = hf is not None
    ts = min(RG_ROWS, s_len)
    nt = s_len // ts
    cb = col0 // b_w
    rows8 = ts // SUBLANES

    def tile(i):
        return nt - 1 - i if rev else i

    in_specs = [pl.BlockSpec((None, ts, b_w), lambda b, i: (b, tile(i), cb)),
                pl.BlockSpec((None, SUBLANES, b_w), lambda b, i: (b, jnp.maximum(tile(i) * rows8 - 1, 0), cb)),
                pl.BlockSpec((None, SUBLANES, b_w),
                             lambda b, i: (b, jnp.minimum((tile(i) + 1) * rows8, s_len // SUBLANES - 1), cb)),
                pl.BlockSpec((4, b_w), lambda b, i: (0, 0)),
                pl.BlockSpec((1, b_w), lambda b, i: (0, 0)),
                pl.BlockSpec(w_dir.shape, lambda b, i: (0, 0, 0)),
                pl.BlockSpec((2, b_w), lambda b, i: (0, 0)),
                pl.BlockSpec((1, b_w), lambda b, i: (0, 0))]
    args = [u3, u3, u3, conv_w, conv_b.reshape(1, b_w), w_dir, bias_dir, lam_dir.reshape(1, b_w)]
    if rev:
        in_specs += [pl.BlockSpec((None, ts, b_w), lambda b, i: (b, tile(i), cb + 1)),
                     pl.BlockSpec((None, ts, b_w), lambda b, i: (b, tile(i), 0))]
        args += [u3, hf]
    return pl.pallas_call(
        functools.partial(_rglru_kernel, rev=rev),
        grid=(bsz, nt),
        in_specs=in_specs,
        out_specs=pl.BlockSpec((None, ts, b_w), lambda b, i: (b, tile(i), 0)),
        out_shape=jax.ShapeDtypeStruct((bsz, s_len, b_w), BF16 if rev else F32),
        scratch_shapes=[pltpu.VMEM((1, b_w), F32)],
        compiler_params=_params(("parallel", "arbitrary"), 2 * 4 * ts * b_w * 4 + (24 << 20)),
        name="rglru_bwd" if rev else "rglru_fwd",
    )(*args)


def _attn_prep_kernel(q0_ref, q1_ref, kv_ref, cos_ref, sa_ref, sb_ref, qn_ref, kn_ref,
                      qo_ref, ko_ref, vo_ref):
    cos = cos_ref[...]
    sin_a = sa_ref[...]
    sin_b = sb_ref[...]

    def norm_rope(t, w):
        y = t * lax.rsqrt(jnp.mean(t * t, axis=-1, keepdims=True) + RMS_EPS) * w
        lane = lax.broadcasted_iota(jnp.int32, y.shape, 1)
        partner = jnp.take_along_axis(y, lane ^ (ROPE_HALF // 2), axis=1)
        return y * cos + partner * (sin_a + sin_b)

    half_heads = q0_ref.shape[1] // HEAD_DIM
    kv_heads = ko_ref.shape[1] // HEAD_DIM
    qn = qn_ref[...]
    kn = kn_ref[...]
    for hd in range(half_heads):
        sl = slice(hd * HEAD_DIM, (hd + 1) * HEAD_DIM)
        so = slice((half_heads + hd) * HEAD_DIM, (half_heads + hd + 1) * HEAD_DIM)
        qo_ref[:, sl] = norm_rope(q0_ref[:, sl], qn).astype(BF16)
        qo_ref[:, so] = norm_rope(q1_ref[:, sl], qn).astype(BF16)
    for hd in range(kv_heads):
        sl = slice(hd * HEAD_DIM, (hd + 1) * HEAD_DIM)
        ko_ref[:, sl] = norm_rope(kv_ref[:, sl], kn).astype(BF16)
    vo_ref[...] = kv_ref[:, kv_heads * HEAD_DIM:].T.astype(BF16)


def _attn_prep(u2, col0, c_w, kv_w, s_len, cos, sin_a, sin_b, qn_w, kn_w):
    t = u2.shape[0]
    tr = min(PREP_ROWS, s_len)
    half = c_w // 2
    assert col0 % half == 0 and 2 * kv_w == half
    cb = col0 // half
    npos = s_len // tr
    tab = pl.BlockSpec((tr, HEAD_DIM), lambda i: (i % npos, 0))
    vec = pl.BlockSpec((1, HEAD_DIM), lambda i: (0, 0))
    return pl.pallas_call(
        _attn_prep_kernel,
        grid=(t // tr,),
        in_specs=[pl.BlockSpec((tr, half), lambda i: (i, cb)),
                  pl.BlockSpec((tr, half), lambda i: (i, cb + 1)),
                  pl.BlockSpec((tr, half), lambda i: (i, cb + 2)),
                  tab, tab, tab, vec, vec],
        out_specs=[pl.BlockSpec((tr, c_w), lambda i: (i, 0)),
                   pl.BlockSpec((tr, kv_w), lambda i: (i, 0)),
                   pl.BlockSpec((kv_w, tr), lambda i: (0, i))],
        out_shape=[jax.ShapeDtypeStruct((t, c_w), BF16),
                   jax.ShapeDtypeStruct((t, kv_w), BF16),
                   jax.ShapeDtypeStruct((kv_w, t), BF16)],
        compiler_params=_params(("parallel",), 2 * tr * (3 * half * 4 + (c_w + 2 * kv_w) * 2) + (16 << 20)),
        name="attn_prep",
    )(u2, u2, u2, cos, sin_a, sin_b, qn_w.reshape(1, HEAD_DIM), kn_w.reshape(1, HEAD_DIM))


def _attn_kernel(q_ref, k_ref, vt_ref, o_ref, sc_scr):
    k = k_ref[...]
    vt = vt_ref[...]
    vt_aug = jnp.concatenate([vt, jnp.ones((BF16_ROWS, vt.shape[1]), BF16)], axis=0)
    s_len = k.shape[0]
    kc = min(ATTN_KB, s_len)
    n_kc = s_len // kc

    def scores(g, c, run_max):
        rows = slice(c * kc, (c + 1) * kc)
        sc = _dot_nt(k[rows, :], q_ref[:, g * HEAD_DIM:(g + 1) * HEAD_DIM])
        sc_scr[g % 2, rows, :] = sc
        cmax = jnp.max(sc.reshape(kc // SUBLANES, SUBLANES, sc.shape[1]), axis=0)
        return cmax if run_max is None else jnp.maximum(run_max, cmax)

    next_max = None
    for c in range(n_kc):
        next_max = scores(0, c, next_max)
    for g in range(C_GROUP):
        mx = jnp.max(next_max, axis=0, keepdims=True)
        next_max = None
        acc = None
        for c in range(n_kc):
            rows = slice(c * kc, (c + 1) * kc)
            p = jnp.exp2((sc_scr[g % 2, rows, :] - mx) * (ATTN_SCALE * LOG2_E)).astype(BF16)
            if g + 1 < C_GROUP:
                next_max = scores(g + 1, c, next_max)
            part = _dot(vt_aug[:, rows], p)
            acc = part if acc is None else acc + part
        ot = acc[:HEAD_DIM, :] / acc[HEAD_DIM:HEAD_DIM + 1, :]
        o_ref[:, g * HEAD_DIM:(g + 1) * HEAD_DIM] = ot.T.astype(o_ref.dtype)


def _attention(qr, kr, vt, bsz, s_len):
    t, c_w = qr.shape
    kv_heads = kr.shape[1] // HEAD_DIM
    gw = C_GROUP * HEAD_DIM
    tq = min(ATTN_TQ, s_len)
    nq = s_len // tq
    return pl.pallas_call(
        _attn_kernel,
        grid=(bsz, kv_heads, nq),
        in_specs=[pl.BlockSpec((tq, gw), lambda b, h, i: (b * nq + i, h)),
                  pl.BlockSpec((s_len, HEAD_DIM), lambda b, h, i: (b, h)),
                  pl.BlockSpec((HEAD_DIM, s_len), lambda b, h, i: (h, b))],
        out_specs=pl.BlockSpec((tq, gw), lambda b, h, i: (b * nq + i, h)),
        out_shape=jax.ShapeDtypeStruct((t, c_w), BF16),
        scratch_shapes=[pltpu.VMEM((2, s_len, tq), F32)],
        compiler_params=_params(("parallel", "parallel", "arbitrary"),
                                8 * tq * s_len * 4 + 4 * s_len * HEAD_DIM * 2 + (8 << 20)),
        name="attention",
    )(qr, kr, vt)


def _outproj_kernel(ya_ref, yb_ref, yc_ref, wa_ref, wb_ref, wc_ref, res_ref, o_ref, *, alpha):
    acc = _dot(ya_ref[...], wa_ref[...])
    acc = acc + _dot(yb_ref[...], wb_ref[...])
    acc = acc + _dot(yc_ref[...], wc_ref[...])
    o_ref[...] = alpha * res_ref[...] + acc


def _outproj(ya, yb, yc, w_out, layer, res, alpha):
    t, a_w = ya.shape
    b_w = yb.shape[1]
    c_w = yc.shape[1]
    n = w_out.shape[2]
    assert a_w == b_w and (a_w + b_w) % c_w == 0
    tm = min(OUT_TM, t)
    tn = min(MM_TN, n)
    c_blk = (a_w + b_w) // c_w
    vm = 2 * (tm * (a_w + b_w + c_w) * 2 + (a_w + b_w + c_w) * tn * 2 + 2 * tm * tn * 4) + tm * tn * 4 + (4 << 20)
    return pl.pallas_call(
        functools.partial(_outproj_kernel, alpha=alpha),
        grid=(t // tm, n // tn),
        in_specs=[pl.BlockSpec((tm, a_w), lambda i, j: (i, 0)),
                  pl.BlockSpec((tm, b_w), lambda i, j: (i, 0)),
                  pl.BlockSpec((tm, c_w), lambda i, j: (i, 0)),
                  pl.BlockSpec((None, a_w, tn), lambda i, j: (layer, 0, j)),
                  pl.BlockSpec((None, b_w, tn), lambda i, j: (layer, 1, j)),
                  pl.BlockSpec((None, c_w, tn), lambda i, j: (layer, c_blk, j)),
                  pl.BlockSpec((tm, tn), lambda i, j: (i, j))],
        out_specs=pl.BlockSpec((tm, tn), lambda i, j: (i, j)),
        out_shape=jax.ShapeDtypeStruct((t, n), F32),
        compiler_params=_params(("parallel", "arbitrary"), vm),
        name="outproj",
    )(ya, yb, yc, w_out, w_out, w_out, res)


def _ffn_up_kernel(x_ref, ex_ref, wg_ref, wu_ref, cw_ref, cb_ref, o_ref, wg_scr, wu_scr, edge_scr,
                   *, tiles_per_seq):
    tm, tn = o_ref.shape
    i = pl.program_id(1)

    @pl.when(i == 0)
    def _():
        wg_scr[...] = wg_ref[...].astype(BF16)
        wu_scr[...] = wu_ref[...].astype(BF16)
        edge_scr[...] = _dot(ex_ref[...], wg_scr[...])

    x = x_ref[...]
    g = _dot(x, wg_scr[...])
    up = _dot(x, wu_scr[...])
    pos = i % tiles_per_seq
    has_prev = (pos > 0).astype(F32)
    has_next = (pos < tiles_per_seq - 1).astype(F32)
    last_edge = edge_scr.shape[0] - 1
    g_prev = edge_scr[pl.ds(jnp.maximum(2 * i - 1, 0), 1), :] * has_prev
    g_next = edge_scr[pl.ds(jnp.minimum(2 * i + 2, last_edge), 1), :] * has_next
    row = lax.broadcasted_iota(jnp.int32, (tm, tn), 0)
    g_m1 = jnp.where(row == 0, g_prev, pltpu.roll(g, 1, 0))
    g_p1 = jnp.where(row == tm - 1, g_next, pltpu.roll(g, tm - 1, 0))
    y = cb_ref[...] + cw_ref[0:1, :] * g_m1 + cw_ref[1:2, :] * g + cw_ref[2:3, :] * g_p1
    o_ref[...] = ((y * _sigmoid(y)) * up).astype(o_ref.dtype)


def _ffn_up(hbf, w_up, layer, conv_w, conv_b, s_len):
    t, d = hbf.shape
    d_ff = w_up.shape[2] // 2
    tm = min(FFN_TM, s_len)
    tn = FFN_TN
    assert d_ff % tn == 0
    nj = d_ff // tn
    n_tiles = t // tm
    tiles_per_seq = s_len // tm
    h3 = hbf.reshape(n_tiles, tm, d)
    edge_x = jnp.stack([h3[:, 0, :], h3[:, tm - 1, :]], axis=1).reshape(2 * n_tiles, d)
    vm = 2 * (tm * d * 2 + 2 * d * tn * 4 + tm * tn * 2) + 2 * d * tn * 2 + 8 * tm * tn * 4 + (4 << 20)
    w_scr = pltpu.VMEM((d, tn), BF16)
    return pl.pallas_call(
        functools.partial(_ffn_up_kernel, tiles_per_seq=tiles_per_seq),
        grid=(nj, n_tiles),
        in_specs=[pl.BlockSpec((tm, d), lambda j, i: (i, 0)),
                  pl.BlockSpec((2 * n_tiles, d), lambda j, i: (0, 0)),
                  pl.BlockSpec((None, d, tn), lambda j, i: (layer, 0, j)),
                  pl.BlockSpec((None, d, tn), lambda j, i: (layer, 0, nj + j)),
                  pl.BlockSpec((3, tn), lambda j, i: (0, j)),
                  pl.BlockSpec((1, tn), lambda j, i: (0, j))],
        out_specs=pl.BlockSpec((tm, tn), lambda j, i: (i, j)),
        out_shape=jax.ShapeDtypeStruct((t, d_ff), BF16),
        scratch_shapes=[w_scr, w_scr, pltpu.VMEM((2 * n_tiles, tn), F32)],
        compiler_params=_params(("parallel", "arbitrary"), vm),
        name="ffn_up",
    )(hbf, edge_x, w_up, w_up, conv_w, conv_b.reshape(1, d_ff))


def _rope_tables(s_len):
    rows = s_len // GRID_W
    g_r, g_c = jnp.meshgrid(jnp.arange(rows), jnp.arange(GRID_W), indexing='ij')
    row = g_r.reshape(s_len).astype(F32)
    colp = g_c.reshape(s_len).astype(F32)
    inv_freq = ROPE_THETA ** (-jnp.arange(0, ROPE_HALF, 2, dtype=F32) / ROPE_HALF)
    ang_r = row[:, None] * inv_freq[None, :]
    ang_c = colp[:, None] * inv_freq[None, :]
    ang = jnp.concatenate([ang_r, ang_r, ang_c, ang_c], axis=-1)
    cos, sin = jnp.cos(ang), jnp.sin(ang)
    first_quarter = (jnp.arange(HEAD_DIM) & (ROPE_HALF // 2)) == 0
    sin_a = jnp.where(first_quarter, -sin, 0.0)
    sin_b = jnp.where(first_quarter, 0.0, sin)
    return cos, sin_a, sin_b


def kernel(x, emb_ln_w, emb_ln_b, w_in, hgrn_lb_logits, hgrn_norm_w, rglru_conv_w, rglru_conv_b,
           rglru_wa, rglru_ba, rglru_wx, rglru_bx, rglru_lambda, attn_q_norm_w, attn_k_norm_w,
           w_out, ln1_w, ln1_b, ffn_w_up, ffn_conv_w, ffn_conv_b, ffn_w_down, ln2_w, ln2_b):
    bsz, s_len, d_model = x.shape
    depth = w_in.shape[0]
    t = bsz * s_len
    a_w = hgrn_norm_w.shape[1]
    b_w = rglru_conv_w.shape[2]
    in_cols = w_in.shape[2]
    a_cols = 5 * a_w
    b_cols = 2 * b_w
    c_w = w_out.shape[1] - a_w - b_w
    kv_w = (in_cols - a_cols - b_cols - c_w) // 2
    alpha = (2.0 * depth) ** 0.25

    cos, sin_a, sin_b = _rope_tables(s_len)
    lb_cs = jnp.cumsum(jax.nn.softmax(hgrn_lb_logits.astype(F32), axis=0), axis=0)
    lower_bounds = lb_cs - lb_cs[0:1]

    w_in_bf = w_in.astype(BF16)
    w_out_bf = w_out.astype(BF16)
    w_down_bf = ffn_w_down.astype(BF16)

    h, hbf = _layernorm(x.reshape(t, d_model), emb_ln_w, emb_ln_b)
    for l in range(depth):
        u2 = _matmul(hbf, w_in_bf, l, MM_TM, MM_TN, F32, "in_proj")
        u3 = u2.reshape(bsz, s_len, in_cols)

        ya = _hgrn2(u3, lower_bounds[l], hgrn_norm_w[l], a_w)

        yb = None
        for d in range(2):
            w_dir = jnp.concatenate([rglru_wa[l, d], rglru_wx[l, d]], axis=-1).astype(BF16)
            bias_dir = jnp.stack([rglru_ba[l, d], rglru_bx[l, d]])
            yb = _rglru_dir(u3, a_cols, rglru_conv_w[l], rglru_conv_b[l], w_dir, bias_dir,
                            rglru_lambda[l, d], b_w, hf=yb)
        yb = yb.reshape(t, b_w)

        qr, kr, vt = _attn_prep(u2, a_cols + b_cols, c_w, kv_w, s_len, cos, sin_a, sin_b,
                                attn_q_norm_w[l], attn_k_norm_w[l])
        yc = _attention(qr, kr, vt, bsz, s_len)

        pre = _outproj(ya, yb, yc, w_out_bf, l, h, alpha)
        h, hbf = _layernorm(pre, ln1_w[l], ln1_b[l])

        act = _ffn_up(hbf, ffn_w_up, l, ffn_conv_w[l], ffn_conv_b[l], s_len)
        pre = _matmul_residual(act, w_down_bf, l, h, alpha, DOWN_TM, DOWN_TN, "ffn_down")
        h, hbf = _layernorm(pre, ln2_w[l], ln2_b[l])
    return h.reshape(bsz, s_len, d_model)
```
